```python
import jax, jax.numpy as jnp
from jax import lax
import numpy as np

D_MODEL = 1024
BATCH = 4
SEQ = 8192
DEPTH = 4

HEAD_DIM = 64
A_Q_HEADS = 8
A_KV_HEADS = 2
B_GROUP_CFG = ((128, 1), (512, 4), (2048, 16))
B_HEADS_PER_GROUP = 4
B_GROUPS = len(B_GROUP_CFG)
B_HEADS = B_GROUPS * B_HEADS_PER_GROUP
POOL_WINDOWS = (2, 4, 8, 16)
POOL_GROUPS = len(POOL_WINDOWS)
POOL_GROUP_WIDTH = 128
POOL_WIDTH = POOL_GROUPS * POOL_GROUP_WIDTH
N_BRANCHES = 3
D_FF = ((-(-8 * D_MODEL // 3) + 255) // 256) * 256
GRID_W = 64
ROPE_THETA = 10000.0
BLOCK = 128
EPS = 1e-6
NEG_INF = -1e30

A_Q_WIDTH = A_Q_HEADS * HEAD_DIM
A_KV_WIDTH = A_KV_HEADS * HEAD_DIM
B_WIDTH = B_HEADS * HEAD_DIM
B_OUT_WIDTH = B_HEADS_PER_GROUP * HEAD_DIM
GATE_WIDTH = N_BRANCHES * D_MODEL
IN_WIDTH = A_Q_WIDTH + 2 * A_KV_WIDTH + 3 * B_WIDTH + POOL_WIDTH + GATE_WIDTH
SPLIT_POINTS = (
    A_Q_WIDTH,
    A_Q_WIDTH + A_KV_WIDTH,
    A_Q_WIDTH + 2 * A_KV_WIDTH,
    A_Q_WIDTH + 2 * A_KV_WIDTH + B_WIDTH,
    A_Q_WIDTH + 2 * A_KV_WIDTH + 2 * B_WIDTH,
    A_Q_WIDTH + 2 * A_KV_WIDTH + 3 * B_WIDTH,
    A_Q_WIDTH + 2 * A_KV_WIDTH + 3 * B_WIDTH + POOL_WIDTH,
)
RESID_SCALE = (2 * DEPTH) ** -0.5

kernel_name = 'hybrid_gated_axial_dilated_pool_encoder'


def rms_norm(x, g):
    xf = x.astype(jnp.float32)
    y = xf * lax.rsqrt(jnp.mean(xf * xf, axis=-1, keepdims=True) + EPS)
    return (y * g.astype(jnp.float32)).astype(x.dtype)


def rope_angles(pos, dim):
    inv = ROPE_THETA ** (-jnp.arange(0, dim, 2, dtype=jnp.float32) / dim)
    return pos.astype(jnp.float32)[:, None] * inv[None, :]


def apply_rope(x, ang):
    xf = x.astype(jnp.float32)
    x1, x2 = jnp.split(xf, 2, axis=-1)
    c = jnp.cos(ang)[None, :, None, :]
    s = jnp.sin(ang)[None, :, None, :]
    return jnp.concatenate([x1 * c - x2 * s, x2 * c + x1 * s], axis=-1).astype(x.dtype)


def apply_axial_rope(x, ang_row, ang_col):
    xr, xc = jnp.split(x, 2, axis=-1)
    return jnp.concatenate([apply_rope(xr, ang_row), apply_rope(xc, ang_col)], axis=-1)


def dense_gqa_blocked(q, k, v):
    b, s, hq, dh = q.shape
    hkv = k.shape[2]
    g = hq // hkv
    nb = s // BLOCK
    scale = dh ** -0.5
    qb = q.reshape(b, nb, BLOCK, hkv, g, dh).transpose(1, 0, 2, 3, 4, 5)

    def one_block(q_blk):
        sc = jnp.einsum('bqkgd,bskd->bkgqs', q_blk, k, preferred_element_type=jnp.float32) * scale
        p = jax.nn.softmax(sc, axis=-1)
        return jnp.einsum('bkgqs,bskd->bqkgd', p.astype(v.dtype), v)

    o = lax.map(one_block, qb)
    return o.transpose(1, 0, 2, 3, 4, 5).reshape(b, s, hq * dh)


def dilated_window_attention(q, k, v, dilation, half_span):
    b, s, h, dh = q.shape
    L = s // dilation
    nb = -(-L // BLOCK)
    lp = nb * BLOCK
    bd = b * dilation

    def to_sub(t):
        t = t.reshape(b, L, dilation, h, dh).transpose(0, 2, 1, 3, 4).reshape(bd, L, h, dh)
        return jnp.pad(t, ((0, 0), (0, lp - L), (0, 0), (0, 0)))

    def band(t):
        tp = jnp.pad(t, ((0, 0), (BLOCK, BLOCK), (0, 0), (0, 0)))
        return jnp.concatenate(
            [tp[:, i * BLOCK:i * BLOCK + lp].reshape(bd, nb, BLOCK, h, dh) for i in range(3)], axis=2)

    qb = to_sub(q).reshape(bd, nb, BLOCK, h, dh)
    kb = band(to_sub(k))
    vb = band(to_sub(v))
    sc = jnp.einsum('znqhd,znkhd->znhqk', qb, kb, preferred_element_type=jnp.float32) * (dh ** -0.5)
    blk = jnp.arange(nb)[:, None, None] * BLOCK
    qpos = blk + jnp.arange(BLOCK)[None, :, None]
    kpos = blk - BLOCK + jnp.arange(3 * BLOCK)[None, None, :]
    valid = (jnp.abs(qpos - kpos) <= half_span) & (kpos >= 0) & (kpos < L)
    sc = jnp.where(valid[None, :, None], sc, NEG_INF)
    m = jnp.max(sc, axis=-1, keepdims=True)
    p = jnp.exp(sc - m)
    den = jnp.sum(p, axis=-1, keepdims=True)
    o = jnp.einsum('znhqk,znkhd->znqhd', (p / den).astype(v.dtype), vb)
    lse = (m + jnp.log(den))[..., 0]
    o = o.reshape(b, dilation, lp, h, dh)[:, :, :L].transpose(0, 2, 1, 3, 4).reshape(b, s, h, dh)
    lse = lse.transpose(0, 1, 3, 2).reshape(b, dilation, lp, h)[:, :, :L]
    lse = lse.transpose(0, 2, 1, 3).reshape(b, s, h)
    return o, lse


def multiscale_pool(u, lin, scale):
    b, s, _ = u.shape
    uf = u.astype(jnp.float32).reshape(b, s, POOL_GROUPS, POOL_GROUP_WIDTH)
    cs = jnp.pad(lax.cumsum(uf, axis=1), ((0, 0), (1, 0), (0, 0), (0, 0)))
    t = jnp.arange(s)
    outs = []
    for gi, w in enumerate(POOL_WINDOWS):
        lo = jnp.clip(t - w // 2, 0, s)
        hi = jnp.clip(t + w - w // 2, 0, s)
        mean = (cs[:, hi, gi] - cs[:, lo, gi]) / (hi - lo).astype(jnp.float32)[None, :, None]
        outs.append(mean - uf[:, :, gi])
    pooled = jnp.stack(outs, axis=2).astype(u.dtype)
    mixed = jnp.einsum('bsgc,gcd->bsgd', pooled, lin)
    return mixed.reshape(b, s, POOL_WIDTH) * scale


def setup_inputs(seed: int = 0) -> dict:
    key = jax.random.key(seed)
    ks = jax.random.split(key, 19)
    f32 = jnp.float32

    def nrm(k, shape, fan_in):
        return jax.random.normal(k, shape, f32) * (fan_in ** -0.5)

    def gain(k, shape):
        return 1.0 + 0.02 * jax.random.normal(k, shape, f32)

    return {
        'x': jax.random.normal(ks[0], (BATCH, SEQ, D_MODEL), f32),
        'norm_mix': gain(ks[1], (DEPTH, D_MODEL)),
        'w_in': nrm(ks[2], (DEPTH, D_MODEL, IN_WIDTH), D_MODEL),
        'b_gate': 0.02 * jax.random.normal(ks[3], (DEPTH, GATE_WIDTH), f32),
        'qn_a': gain(ks[4], (DEPTH, HEAD_DIM)),
        'kn_a': gain(ks[5], (DEPTH, HEAD_DIM)),
        'qn_b': gain(ks[6], (DEPTH, HEAD_DIM)),
        'kn_b': gain(ks[7], (DEPTH, HEAD_DIM)),
        'pool_lin': nrm(ks[8], (DEPTH, POOL_GROUPS, POOL_GROUP_WIDTH, POOL_GROUP_WIDTH), POOL_GROUP_WIDTH),
        'pool_scale': gain(ks[9], (DEPTH, POOL_WIDTH)),
        'w_branch_a': nrm(ks[10], (DEPTH, A_Q_WIDTH, D_MODEL), A_Q_WIDTH),
        'w_branch_b': nrm(ks[11], (DEPTH, B_OUT_WIDTH, D_MODEL), B_OUT_WIDTH),
        'w_branch_c': nrm(ks[12], (DEPTH, POOL_WIDTH, D_MODEL), POOL_WIDTH),
        'w_out': nrm(ks[13], (DEPTH, D_MODEL, D_MODEL), D_MODEL) * RESID_SCALE,
        'norm_ffn': gain(ks[14], (DEPTH, D_MODEL)),
        'w_ffn_gate': nrm(ks[15], (DEPTH, D_MODEL, D_FF), D_MODEL),
        'w_ffn_up': nrm(ks[16], (DEPTH, D_MODEL, D_FF), D_MODEL),
        'w_ffn_down': nrm(ks[17], (DEPTH, D_FF, D_MODEL), D_FF) * RESID_SCALE,
    }


def reference(x, norm_mix, w_in, b_gate, qn_a, kn_a, qn_b, kn_b, pool_lin, pool_scale,
              w_branch_a, w_branch_b, w_branch_c, w_out, norm_ffn, w_ffn_gate, w_ffn_up, w_ffn_down):
    b, s, _ = x.shape
    rows = s // GRID_W
    row_idx = jnp.repeat(jnp.arange(rows), GRID_W)
    col_idx = jnp.tile(jnp.arange(GRID_W), rows)
    ang_row = rope_angles(row_idx, HEAD_DIM // 2)
    ang_col = rope_angles(col_idx, HEAD_DIM // 2)
    ang_seq = rope_angles(jnp.arange(s), HEAD_DIM)

    for l in range(DEPTH):
        h = rms_norm(x, norm_mix[l])
        z = h @ w_in[l]
        qa, ka, va, qb, kb, vb, uc, gz = jnp.split(z, SPLIT_POINTS, axis=-1)

        qa = apply_axial_rope(rms_norm(qa.reshape(b, s, A_Q_HEADS, HEAD_DIM), qn_a[l]), ang_row, ang_col)
        ka = apply_axial_rope(rms_norm(ka.reshape(b, s, A_KV_HEADS, HEAD_DIM), kn_a[l]), ang_row, ang_col)
        va = va.reshape(b, s, A_KV_HEADS, HEAD_DIM)
        ya = dense_gqa_blocked(qa, ka, va) @ w_branch_a[l]

        qb = apply_rope(rms_norm(qb.reshape(b, s, B_HEADS, HEAD_DIM), qn_b[l]), ang_seq)
        kb = apply_rope(rms_norm(kb.reshape(b, s, B_HEADS, HEAD_DIM), kn_b[l]), ang_seq)
        qb = qb.reshape(b, s, B_GROUPS, B_HEADS_PER_GROUP, HEAD_DIM)
        kb = kb.reshape(b, s, B_GROUPS, B_HEADS_PER_GROUP, HEAD_DIM)
        vb = vb.reshape(b, s, B_GROUPS, B_HEADS_PER_GROUP, HEAD_DIM)
        o_list, lse_list = [], []
        for gi, (window, dil) in enumerate(B_GROUP_CFG):
            o_g, lse_g = dilated_window_attention(qb[:, :, gi], kb[:, :, gi], vb[:, :, gi], dil, window // (2 * dil))
            o_list.append(o_g)
            lse_list.append(lse_g)
        wts = jax.nn.softmax(jnp.stack(lse_list, axis=0), axis=0)
        ob = jnp.einsum('gbsh,gbshd->bshd', wts.astype(x.dtype), jnp.stack(o_list, axis=0))
        yb = ob.reshape(b, s, B_OUT_WIDTH) @ w_branch_b[l]

        yc = multiscale_pool(uc, pool_lin[l], pool_scale[l]) @ w_branch_c[l]

        gates = jax.nn.sigmoid(gz + b_gate[l]).reshape(b, s, N_BRANCHES, D_MODEL)
        merged = gates[:, :, 0] * ya + gates[:, :, 1] * yb + gates[:, :, 2] * yc
        x = x + merged @ w_out[l]

        h2 = rms_norm(x, norm_ffn[l])
        x = x + (jax.nn.silu(h2 @ w_ffn_gate[l]) * (h2 @ w_ffn_up[l])) @ w_ffn_down[l]
    return x
```

```python
import functools
import math

import jax
import jax.numpy as jnp
from jax import lax
from jax.experimental import pallas as pl
from jax.experimental.pallas import tpu as pltpu

F32 = jnp.float32
BF16 = jnp.bfloat16

HEAD_DIM = 64
A_Q_HEADS = 8
A_KV_HEADS = 2
A_GROUP = A_Q_HEADS // A_KV_HEADS
B_GROUP_CFG = ((128, 1), (512, 4), (2048, 16))
B_HEADS_PER_GROUP = 4
B_GROUPS = len(B_GROUP_CFG)
POOL_WINDOWS = (2, 4, 8, 16)
POOL_GROUP_WIDTH = 128
GRID_W = 64
ROPE_THETA = 10000.0
EPS = 1e-6
NEG_INF = -1e30

A_Q_WIDTH = A_Q_HEADS * HEAD_DIM
A_KV_WIDTH = A_KV_HEADS * HEAD_DIM
B_GROUP_WIDTH = B_HEADS_PER_GROUP * HEAD_DIM
B_WIDTH = B_GROUPS * B_GROUP_WIDTH
POOL_WIDTH = len(POOL_WINDOWS) * POOL_GROUP_WIDTH
MAIN_WIDTH = A_Q_WIDTH + 2 * A_KV_WIDTH + 3 * B_WIDTH + POOL_WIDTH
OFF_QA = 0
OFF_KA = OFF_QA + A_Q_WIDTH
OFF_VA = OFF_KA + A_KV_WIDTH
OFF_QB = OFF_VA + A_KV_WIDTH
OFF_KB = OFF_QB + B_WIDTH
OFF_VB = OFF_KB + B_WIDTH
OFF_UC = OFF_VB + B_WIDTH

Q_SCALE = HEAD_DIM ** -0.5 * math.log2(math.e)

LANES = 128
SUBLANES = 8
MXU_DIM = 256
POOL_HALO = SUBLANES
VMEM_LIMIT_BYTES = 56 * 1024 * 1024

PROJ_TM = 256
ATTN_A_TQ = 256
ATTN_A_TK = 512
ATTN_B_TL = 512
ATTN_B_QC = 128
MERGE_TM = 256
FFN_TM = 256


def _params(n_axes):
    return pltpu.CompilerParams(dimension_semantics=("arbitrary",) * n_axes,
                                vmem_limit_bytes=VMEM_LIMIT_BYTES)


def _resident(block_shape, index_map):
    return pl.BlockSpec(block_shape, index_map, pipeline_mode=pl.Buffered(1))


def _rms_norm_rows(x, gain):
    ms = jnp.mean(x * x, axis=-1, keepdims=True)
    return (x * lax.rsqrt(ms + EPS)) * gain


def _dot(a, b):
    return jnp.dot(a, b, preferred_element_type=F32)


def _head_norm_rope(zs, ones_ref, gain, cos, sin, half, out_scale):
    tm, w = zs.shape
    lane = lax.broadcasted_iota(jnp.int32, (tm, LANES), 1)
    first = (lane & half) == 0
    pieces = []
    for c0 in range(0, w, MXU_DIM):
        cw = min(MXU_DIM, w - c0)
        zc = zs[:, c0:c0 + cw]
        z2 = zc * zc
        hi = z2.astype(BF16)
        lo = (z2 - hi.astype(F32)).astype(BF16)
        ones = ones_ref[:cw, :cw]
        ss = _dot(hi, ones) + _dot(lo, ones)
        r = lax.rsqrt(ss * (1.0 / HEAD_DIM) + EPS)
        for p0 in range(0, cw, LANES):
            xn = (zc[:, p0:p0 + LANES] * r[:, p0:p0 + LANES]) * gain
            partner = jnp.where(first, pltpu.roll(xn, LANES - half, 1), pltpu.roll(xn, half, 1))
            o = xn * cos + partner * sin
            if out_scale != 1.0:
                o = o * out_scale
            pieces.append(o)
    return pieces


def _proj_kernel(x_ref, gmix_ref, w_ref, ones_ref, gqa_ref, gka_ref, gqb_ref, gkb_ref,
                 ca_ref, sa_ref, cb_ref, sb_ref,
                 qat_ref, ka_ref, vat_ref,
                 qb0_ref, qb1_ref, qb2_ref, kb0_ref, kb1_ref, kb2_ref, vb0_ref, vb1_ref, vb2_ref,
                 uc_ref):
    h = _rms_norm_rows(x_ref[0], gmix_ref[...]).astype(BF16)
    z = _dot(h, w_ref[...])
    ca, sa, cb, sb = ca_ref[...], sa_ref[...], cb_ref[...], sb_ref[...]

    qa = _head_norm_rope(z[:, OFF_QA:OFF_QA + A_Q_WIDTH], ones_ref, gqa_ref[...], ca, sa,
                         HEAD_DIM // 4, Q_SCALE)
    for p, piece in enumerate(qa):
        qat_ref[0, p * LANES:(p + 1) * LANES, :] = piece.T.astype(BF16)
    ka = _head_norm_rope(z[:, OFF_KA:OFF_KA + A_KV_WIDTH], ones_ref, gka_ref[...], ca, sa,
                         HEAD_DIM // 4, 1.0)
    ka_ref[0] = ka[0].astype(BF16)
    vat_ref[0] = z[:, OFF_VA:OFF_VA + A_KV_WIDTH].T.astype(BF16)

    qb = _head_norm_rope(z[:, OFF_QB:OFF_QB + B_WIDTH], ones_ref, gqb_ref[...], cb, sb,
                         HEAD_DIM // 2, Q_SCALE)
    kb = _head_norm_rope(z[:, OFF_KB:OFF_KB + B_WIDTH], ones_ref, gkb_ref[...], cb, sb,
                         HEAD_DIM // 2, 1.0)
    for g, (q_ref, k_ref, v_ref) in enumerate(((qb0_ref, kb0_ref, vb0_ref),
                                               (qb1_ref, kb1_ref, vb1_ref),
                                               (qb2_ref, kb2_ref, vb2_ref))):
        q_ref[0] = jnp.concatenate(qb[2 * g:2 * g + 2], axis=1).astype(BF16)
        k_ref[0] = jnp.concatenate(kb[2 * g:2 * g + 2], axis=1).astype(BF16)
        c0 = OFF_VB + g * B_GROUP_WIDTH
        v_ref[0] = z[:, c0:c0 + B_GROUP_WIDTH].astype(BF16)
    uc_ref[0] = z[:, OFF_UC:OFF_UC + POOL_WIDTH]


def _proj(x, layer, gmix, w_main, ones_blk, gqa, gka, gqb, gkb, ca, sa, cb, sb):
    b, s, d = x.shape
    tm = PROJ_TM
    grid = (b, s // tm)
    row = lambda bi, i: (bi, i, 0)
    lay = lambda bi, i: (layer, 0, 0)
    tab = lambda bi, i: (i, 0)
    bsd = lambda w, dt: jax.ShapeDtypeStruct((b, s, w), dt)
    out_shape = ([jax.ShapeDtypeStruct((b, A_Q_WIDTH, s), BF16), bsd(A_KV_WIDTH, BF16),
                  jax.ShapeDtypeStruct((b, A_KV_WIDTH, s), BF16)]
                 + [bsd(B_GROUP_WIDTH, BF16)] * 9 + [bsd(POOL_WIDTH, F32)])
    out_specs = ([pl.BlockSpec((1, A_Q_WIDTH, tm), lambda bi, i: (bi, 0, i)),
                  pl.BlockSpec((1, tm, A_KV_WIDTH), row),
                  pl.BlockSpec((1, A_KV_WIDTH, tm), lambda bi, i: (bi, 0, i))]
                 + [pl.BlockSpec((1, tm, B_GROUP_WIDTH), row)] * 9
                 + [pl.BlockSpec((1, tm, POOL_WIDTH), row)])
    gain_spec = _resident((None, 1, LANES), lay)
    tab_spec = pl.BlockSpec((tm, LANES), tab)
    in_specs = [pl.BlockSpec((1, tm, d), row),
                _resident((None, 1, d), lay),
                _resident((None, d, MAIN_WIDTH), lay),
                _resident((MXU_DIM, MXU_DIM), lambda bi, i: (0, 0)),
                gain_spec, gain_spec, gain_spec, gain_spec,
                tab_spec, tab_spec, tab_spec, tab_spec]
    return pl.pallas_call(
        _proj_kernel, grid=grid, in_specs=in_specs, out_specs=out_specs, out_shape=out_shape,
        compiler_params=_params(2), name="proj",
    )(x, gmix, w_main, ones_blk, gqa, gka, gqb, gkb, ca, sa, cb, sb)


def _attn_a_kernel(qt_ref, k_ref, vt_ref, o_ref, acc_ref, *, tq, tk):
    s_len = k_ref.shape[1]
    nq = A_GROUP * tq
    for g in range(A_KV_HEADS):
        heads = [qt_ref[0, (g * A_GROUP + j) * HEAD_DIM:(g * A_GROUP + j + 1) * HEAD_DIM, :]
                 for j in range(A_GROUP)]
        qg = jnp.concatenate(heads, axis=1)
        zeros = jnp.zeros_like(qg)
        parts = [zeros] * A_KV_HEADS
        parts[g] = qg
        rhs = jnp.concatenate(parts, axis=0)

        def body(c, carry, rhs=rhs, g=g):
            m, l, acc = carry
            start = pl.multiple_of(c * tk, tk)
            sc = _dot(k_ref[0, pl.ds(start, tk), :], rhs)
            m_new = jnp.maximum(m, jnp.max(sc, axis=0, keepdims=True))
            alpha = jnp.exp2(m - m_new)
            p = jnp.exp2(sc - m_new)
            l = alpha * l + jnp.sum(p, axis=0, keepdims=True)
            vt = vt_ref[0, g * HEAD_DIM:(g + 1) * HEAD_DIM, pl.ds(start, tk)]
            acc = alpha * acc + _dot(vt, p.astype(BF16))
            return m_new, l, acc

        init = (jnp.full((1, nq), NEG_INF, F32), jnp.zeros((1, nq), F32),
                jnp.zeros((HEAD_DIM, nq), F32))
        _, l, acc = lax.fori_loop(0, s_len // tk, body, init)
        o = acc / l
        for j in range(A_GROUP):
            hd = (g * A_GROUP + j) * HEAD_DIM
            acc_ref[hd:hd + HEAD_DIM, :] = o[:, j * tq:(j + 1) * tq]
    o_ref[0] = acc_ref[...].T.astype(BF16)


def _attn_a(qat, ka, vat):
    b, _, s = qat.shape
    tq, tk = ATTN_A_TQ, ATTN_A_TK
    return pl.pallas_call(
        functools.partial(_attn_a_kernel, tq=tq, tk=tk),
        grid=(b, s // tq),
        in_specs=[pl.BlockSpec((1, A_Q_WIDTH, tq), lambda bi, i: (bi, 0, i)),
                  pl.BlockSpec((1, s, A_KV_WIDTH), lambda bi, i: (bi, 0, 0)),
                  pl.BlockSpec((1, A_KV_WIDTH, s), lambda bi, i: (bi, 0, 0))],
        out_specs=pl.BlockSpec((1, tq, A_Q_WIDTH), lambda bi, i: (bi, i, 0)),
        out_shape=jax.ShapeDtypeStruct((b, s, A_Q_WIDTH), BF16),
        scratch_shapes=[pltpu.VMEM((A_Q_WIDTH, tq), F32)],
        compiler_params=_params(2), name="attn_a",
    )(qat, ka, vat)


def _attn_b_kernel(q_ref, k_ref, v_ref, o_ref, lse_ref, *, tl, half_span):
    sub_len = k_ref.shape[1]
    qc = ATTN_B_QC
    win = qc + 2 * half_span
    lt = pl.program_id(2)
    for c in range(tl // qc):
        row0 = lt * tl + c * qc
        ws = pl.multiple_of(jnp.clip(row0 - half_span, 0, sub_len - win), half_span)
        kw = k_ref[0, pl.ds(ws, win), :]
        vw = v_ref[0, pl.ds(ws, win), :]
        qpos = row0 + lax.broadcasted_iota(jnp.int32, (qc, win), 0)
        kpos = ws + lax.broadcasted_iota(jnp.int32, (qc, win), 1)
        valid = jnp.abs(qpos - kpos) <= half_span
        outs, lses = [], []
        for hh in range(B_HEADS_PER_GROUP):
            cols = slice(hh * HEAD_DIM, (hh + 1) * HEAD_DIM)
            qh = q_ref[0, c * qc:(c + 1) * qc, cols]
            sc = lax.dot_general(qh, kw[:, cols], (((1,), (1,)), ((), ())),
                                 preferred_element_type=F32)
            sc = jnp.where(valid, sc, NEG_INF)
            m = jnp.max(sc, axis=-1, keepdims=True)
            p = jnp.exp2(sc - m)
            l = jnp.sum(p, axis=-1, keepdims=True)
            outs.append(_dot(p.astype(BF16), vw[:, cols]) / l)
            lses.append(jnp.broadcast_to(m + jnp.log2(l), (qc, HEAD_DIM)))
        o_ref[0, c * qc:(c + 1) * qc, :] = jnp.concatenate(outs, axis=1)
        lse_ref[0, c * qc:(c + 1) * qc, :] = jnp.concatenate(lses, axis=1)


def _attn_b(q, k, v, window, dilation):
    b, s, w = q.shape
    sub_len = s // dilation
    half_span = window // (2 * dilation)
    tl = min(ATTN_B_TL, sub_len)
    view = lambda t: t.reshape(b, sub_len, dilation * w)
    tile = pl.BlockSpec((1, tl, w), lambda bi, r, lt: (bi, lt, r))
    whole = pl.BlockSpec((1, sub_len, w), lambda bi, r, lt: (bi, 0, r))
    o, lse = pl.pallas_call(
        functools.partial(_attn_b_kernel, tl=tl, half_span=half_span),
        grid=(b, dilation, sub_len // tl),
        in_specs=[tile, whole, whole],
        out_specs=[tile, tile],
        out_shape=[jax.ShapeDtypeStruct((b, sub_len, dilation * w), F32)] * 2,
        compiler_params=_params(3), name=f"attn_b_d{dilation}",
    )(view(q), view(k), view(v))
    return o.reshape(b, s, w), lse.reshape(b, s, w)


def _merge_kernel(x_ref, oa_ref, ob0_ref, ob1_ref, ob2_ref, ls0_ref, ls1_ref, ls2_ref,
                  ucp_ref, uc_ref, ucn_ref,
                  gmix_ref, wg_ref, bg_ref, wa_ref, wb_ref, wc_ref, lin_ref, psc_ref, wo_ref,
                  out_ref, ext_ref, *, tm, seq_len):
    i = pl.program_id(1)
    d = x_ref.shape[2]
    x = x_ref[0]
    h = _rms_norm_rows(x, gmix_ref[...]).astype(BF16)
    gz = _dot(h, wg_ref[...]) + bg_ref[...]
    gates = 1.0 / (1.0 + jnp.exp(-gz))

    ya = _dot(oa_ref[0], wa_ref[...])

    ls = (ls0_ref[0], ls1_ref[0], ls2_ref[0])
    top = jnp.maximum(jnp.maximum(ls[0], ls[1]), ls[2])
    wts = [jnp.exp2(t - top) for t in ls]
    ob = (wts[0] * ob0_ref[0] + wts[1] * ob1_ref[0] + wts[2] * ob2_ref[0]) / (wts[0] + wts[1] + wts[2])
    yb = _dot(ob.astype(BF16), wb_ref[...])

    halo = POOL_HALO
    ext_ref[0:halo, :] = jnp.where(i > 0, ucp_ref[0], 0.0)
    ext_ref[halo:halo + tm, :] = uc_ref[0]
    ext_ref[halo + tm:2 * halo + tm, :] = jnp.where(i < pl.num_programs(1) - 1, ucn_ref[0], 0.0)
    t = i * tm + lax.broadcasted_iota(jnp.int32, (tm, 1), 0)
    mixed = []
    for gi, w in enumerate(POOL_WINDOWS):
        cols = slice(gi * POOL_GROUP_WIDTH, (gi + 1) * POOL_GROUP_WIDTH)
        tot = None
        for j in range(w):
            r0 = halo - w // 2 + j
            part = ext_ref[r0:r0 + tm, cols]
            tot = part if tot is None else tot + part
        cnt = jnp.minimum(t + (w - w // 2), seq_len) - jnp.maximum(t - w // 2, 0)
        pooled = tot / cnt.astype(F32) - ext_ref[halo:halo + tm, cols]
        mixed.append(_dot(pooled.astype(BF16), lin_ref[gi]) * psc_ref[:, cols])
    yc = _dot(jnp.concatenate(mixed, axis=1).astype(BF16), wc_ref[...])

    merged = gates[:, 0:d] * ya + gates[:, d:2 * d] * yb + gates[:, 2 * d:3 * d] * yc
    out_ref[0] = x + _dot(merged.astype(BF16), wo_ref[...])


def _merge(x, layer, oa, obs, lss, uc, gmix, wg, bg, wa, wb, wc, lin, psc, wo):
    b, s, d = x.shape
    tm = MERGE_TM
    nt = s // tm
    hb = tm // POOL_HALO
    n_hblk = s // POOL_HALO
    row = lambda bi, i: (bi, i, 0)
    lay = lambda bi, i: (layer, 0, 0)
    tile = lambda w: pl.BlockSpec((1, tm, w), row)
    in_specs = ([tile(d), tile(A_Q_WIDTH)] + [tile(B_GROUP_WIDTH)] * 6
                + [pl.BlockSpec((1, POOL_HALO, POOL_WIDTH),
                                lambda bi, i: (bi, jnp.maximum(i * hb - 1, 0), 0)),
                   tile(POOL_WIDTH),
                   pl.BlockSpec((1, POOL_HALO, POOL_WIDTH),
                                lambda bi, i: (bi, jnp.minimum((i + 1) * hb, n_hblk - 1), 0)),
                   _resident((None, 1, d), lay),
                   _resident((None, d, 3 * d), lay),
                   _resident((None, 1, 3 * d), lay),
                   _resident((None, A_Q_WIDTH, d), lay),
                   _resident((None, B_GROUP_WIDTH, d), lay),
                   _resident((None, POOL_WIDTH, d), lay),
                   _resident((None, len(POOL_WINDOWS), POOL_GROUP_WIDTH, POOL_GROUP_WIDTH),
                             lambda bi, i: (layer, 0, 0, 0)),
                   _resident((None, 1, POOL_WIDTH), lay),
                   _resident((None, d, d), lay)])
    return pl.pallas_call(
        functools.partial(_merge_kernel, tm=tm, seq_len=s),
        grid=(b, nt), in_specs=in_specs, out_specs=tile(d),
        out_shape=jax.ShapeDtypeStruct((b, s, d), F32),
        scratch_shapes=[pltpu.VMEM((tm + 2 * POOL_HALO, POOL_WIDTH), F32)],
        compiler_params=_params(2), name="merge",
    )(x, oa, *obs, *lss, uc, uc, uc, gmix, wg, bg, wa, wb, wc, lin, psc, wo)


def _ffn_kernel(x_ref, gn_ref, wgate_ref, wup_ref, wdown_ref, out_ref):
    x = x_ref[0]
    h = _rms_norm_rows(x, gn_ref[...]).astype(BF16)
    a = _dot(h, wgate_ref[...])
    u = _dot(h, wup_ref[...])
    act = (a / (1.0 + jnp.exp(-a))) * u
    out_ref[0] = x + _dot(act.astype(BF16), wdown_ref[...])


def _ffn(x, layer, gn, wgate, wup, wdown):
    b, s, d = x.shape
    dff = wgate.shape[2]
    tm = FFN_TM
    row = lambda bi, i: (bi, i, 0)
    lay = lambda bi, i: (layer, 0, 0)
    return pl.pallas_call(
        _ffn_kernel, grid=(b, s // tm),
        in_specs=[pl.BlockSpec((1, tm, d), row),
                  _resident((None, 1, d), lay),
                  _resident((None, d, dff), lay),
                  _resident((None, d, dff), lay),
                  _resident((None, dff, d), lay)],
        out_specs=pl.BlockSpec((1, tm, d), row),
        out_shape=jax.ShapeDtypeStruct((b, s, d), F32),
        compiler_params=_params(2), name="ffn",
    )(x, gn, wgate, wup, wdown)


def _rope_tables(s):
    def angles(pos, dim):
        inv = ROPE_THETA ** (-jnp.arange(0, dim, 2, dtype=F32) / dim)
        return pos.astype(F32)[:, None] * inv[None, :]

    t = jnp.arange(s)
    ang_row = angles(t // GRID_W, HEAD_DIM // 2)
    ang_col = angles(t % GRID_W, HEAD_DIM // 2)
    ang_seq = angles(t, HEAD_DIM)

    def table(parts):
        return jnp.tile(jnp.concatenate(parts, axis=1), (1, LANES // HEAD_DIM))

    ca = table([jnp.cos(ang_row)] * 2 + [jnp.cos(ang_col)] * 2)
    sa = table([-jnp.sin(ang_row), jnp.sin(ang_row), -jnp.sin(ang_col), jnp.sin(ang_col)])
    cb = table([jnp.cos(ang_seq)] * 2)
    sb = table([-jnp.sin(ang_seq), jnp.sin(ang_seq)])
    return ca, sa, cb, sb


def kernel(x, norm_mix, w_in, b_gate, qn_a, kn_a, qn_b, kn_b, pool_lin, pool_scale,
           w_branch_a, w_branch_b, w_branch_c, w_out, norm_ffn, w_ffn_gate, w_ffn_up, w_ffn_down):
    b, s, d = x.shape
    depth = w_in.shape[0]
    assert w_in.shape[2] == MAIN_WIDTH + 3 * d
    assert s % GRID_W == 0 and s % max(PROJ_TM, ATTN_A_TQ, ATTN_A_TK, MERGE_TM, FFN_TM) == 0
    for window, dil in B_GROUP_CFG:
        assert (s // dil) % ATTN_B_QC == 0 and s // dil >= ATTN_B_QC + window // dil

    ca, sa, cb, sb = _rope_tables(s)
    ones_blk = jnp.kron(jnp.eye(MXU_DIM // HEAD_DIM, dtype=F32),
                        jnp.ones((HEAD_DIM, HEAD_DIM), F32)).astype(BF16)
    row3 = lambda t: t.reshape(depth, 1, t.shape[-1])
    two_heads = lambda t: row3(jnp.tile(t, (1, LANES // HEAD_DIM)))
    w_main = w_in[:, :, :MAIN_WIDTH].astype(BF16)
    w_g = w_in[:, :, MAIN_WIDTH:].astype(BF16)
    gmix, gffn, bg, psc = row3(norm_mix), row3(norm_ffn), row3(b_gate), row3(pool_scale)
    gqa, gka, gqb, gkb = two_heads(qn_a), two_heads(kn_a), two_heads(qn_b), two_heads(kn_b)
    wa, wb, wc, wo = (t.astype(BF16) for t in (w_branch_a, w_branch_b, w_branch_c, w_out))
    lin = pool_lin.astype(BF16)
    wgate, wup, wdown = (t.astype(BF16) for t in (w_ffn_gate, w_ffn_up, w_ffn_down))

    for layer in range(depth):
        (qat, ka, vat, qb0, qb1, qb2, kb0, kb1, kb2, vb0, vb1, vb2, uc) = _proj(
            x, layer, gmix, w_main, ones_blk, gqa, gka, gqb, gkb, ca, sa, cb, sb)
        oa = _attn_a(qat, ka, vat)
        obs, lss = [], []
        for (window, dil), q, k, v in zip(B_GROUP_CFG, (qb0, qb1, qb2), (kb0, kb1, kb2),
                                          (vb0, vb1, vb2)):
            o, lse = _attn_b(q, k, v, window, dil)
            obs.append(o)
            lss.append(lse)
        x = _merge(x, layer, oa, obs, lss, uc, gmix, w_g, bg, wa, wb, wc, lin, psc, wo)
        x = _ffn(x, layer, gffn, wgate, wup, wdown)
    return x
```

```python
import functools
import math

import jax
import jax.numpy as jnp
from jax import lax
from jax.experimental import pallas as pl
from jax.experimental.pallas import tpu as pltpu

F32 = jnp.float32
BF16 = jnp.bfloat16

HEAD_DIM = 64
A_Q_HEADS = 8
A_KV_HEADS = 2
A_GROUP = A_Q_HEADS // A_KV_HEADS
B_GROUP_CFG = ((128, 1), (512, 4), (2048, 16))
B_HEADS_PER_GROUP = 4
B_GROUPS = len(B_GROUP_CFG)
POOL_WINDOWS = (2, 4, 8, 16)
POOL_GROUP_WIDTH = 128
GRID_W = 64
ROPE_THETA = 10000.0
EPS = 1e-6
NEG_INF = -1e30

A_Q_WIDTH = A_Q_HEADS * HEAD_DIM
A_KV_WIDTH = A_KV_HEADS * HEAD_DIM
B_GROUP_WIDTH = B_HEADS_PER_GROUP * HEAD_DIM
B_WIDTH = B_GROUPS * B_GROUP_WIDTH
POOL_WIDTH = len(POOL_WINDOWS) * POOL_GROUP_WIDTH
MAIN_WIDTH = A_Q_WIDTH + 2 * A_KV_WIDTH + 3 * B_WIDTH + POOL_WIDTH
OFF_QA = 0
OFF_KA = OFF_QA + A_Q_WIDTH
OFF_VA = OFF_KA + A_KV_WIDTH
OFF_QB = OFF_VA + A_KV_WIDTH
OFF_KB = OFF_QB + B_WIDTH
OFF_VB = OFF_KB + B_WIDTH
OFF_UC = OFF_VB + B_WIDTH

Q_SCALE = HEAD_DIM ** -0.5 * math.log2(math.e)

LANES = 128
SUBLANES = 8
BF16_ROWS = 16
MXU_DIM = 256
A_VT_ROWS = HEAD_DIM + BF16_ROWS
POOL_HALO = SUBLANES
VMEM_LIMIT_BYTES = 56 * 1024 * 1024

PROJ_TM = 512
ATTN_A_TQ = 256
ATTN_A_TK = 512
ATTN_B_TL = 512
ATTN_B_QC = 128
MERGE_TM = 512
FFN_TM = 512


def _params(n_axes):
    return pltpu.CompilerParams(dimension_semantics=("arbitrary",) * n_axes,
                                vmem_limit_bytes=VMEM_LIMIT_BYTES)


def _resident(block_shape, index_map):
    return pl.BlockSpec(block_shape, index_map, pipeline_mode=pl.Buffered(1))


def _rms_norm_rows(x, gain):
    ms = jnp.mean(x * x, axis=-1, keepdims=True)
    return (x * lax.rsqrt(ms + EPS)) * gain


def _dot(a, b):
    return jnp.dot(a, b, preferred_element_type=F32)


def _head_norm_rope(zs, ones_ref, gain, cos, sin, half, out_scale):
    tm, w = zs.shape
    lane = lax.broadcasted_iota(jnp.int32, (tm, LANES), 1)
    first = (lane & half) == 0
    pieces = []
    for c0 in range(0, w, MXU_DIM):
        cw = min(MXU_DIM, w - c0)
        zc = zs[:, c0:c0 + cw]
        z2 = zc * zc
        hi = z2.astype(BF16)
        lo = (z2 - hi.astype(F32)).astype(BF16)
        ones = ones_ref[:cw, :cw]
        ss = _dot(hi, ones) + _dot(lo, ones)
        r = lax.rsqrt(ss * (1.0 / HEAD_DIM) + EPS)
        for p0 in range(0, cw, LANES):
            xn = (zc[:, p0:p0 + LANES] * r[:, p0:p0 + LANES]) * gain
            partner = jnp.where(first, pltpu.roll(xn, LANES - half, 1), pltpu.roll(xn, half, 1))
            o = xn * cos + partner * sin
            if out_scale != 1.0:
                o = o * out_scale
            pieces.append(o)
    return pieces


def _store_residue_major(halves, scr_ref, out_ref, dil):
    n = halves[0].shape[0] // dil
    for hf, piece in enumerate(halves):
        scr_ref[hf] = piece
    for r in range(dil):
        for hf in range(2):
            c0 = r * B_GROUP_WIDTH + hf * LANES
            out_ref[0, :, c0:c0 + LANES] = scr_ref[hf, pl.ds(r, n, stride=dil), :].astype(BF16)


def _load_token_major(blk_ref, scr_ref, dil):
    n = blk_ref.shape[1]
    for r in range(dil):
        for hf in range(2):
            c0 = r * B_GROUP_WIDTH + hf * LANES
            scr_ref[hf, pl.ds(r, n, stride=dil), :] = blk_ref[0, :, c0:c0 + LANES]
    return jnp.concatenate([scr_ref[0], scr_ref[1]], axis=1)


def _proj_kernel(x_ref, gmix_ref, w_ref, ones_ref, gqa_ref, gka_ref, gqb_ref, gkb_ref,
                 ca_ref, sa_ref, cb_ref, sb_ref,
                 qat_ref, ka_ref, vat_ref,
                 qb0_ref, qb1_ref, qb2_ref, kb0_ref, kb1_ref, kb2_ref, vb0_ref, vb1_ref, vb2_ref,
                 uc_ref, *perm_refs):
    h = _rms_norm_rows(x_ref[0], gmix_ref[...]).astype(BF16)
    z = _dot(h, w_ref[...])
    ca, sa, cb, sb = ca_ref[...], sa_ref[...], cb_ref[...], sb_ref[...]

    qa = _head_norm_rope(z[:, OFF_QA:OFF_QA + A_Q_WIDTH], ones_ref, gqa_ref[...], ca, sa,
                         HEAD_DIM // 4, Q_SCALE)
    for p, piece in enumerate(qa):
        qat_ref[0, p * LANES:(p + 1) * LANES, :] = piece.T.astype(BF16)
    ka = _head_norm_rope(z[:, OFF_KA:OFF_KA + A_KV_WIDTH], ones_ref, gka_ref[...], ca, sa,
                         HEAD_DIM // 4, 1.0)
    ka_ref[0] = ka[0].astype(BF16)
    vat = z[:, OFF_VA:OFF_VA + A_KV_WIDTH].T.astype(BF16)
    for g in range(A_KV_HEADS):
        r0 = g * A_VT_ROWS
        vat_ref[0, r0:r0 + HEAD_DIM, :] = vat[g * HEAD_DIM:(g + 1) * HEAD_DIM]
        vat_ref[0, r0 + HEAD_DIM:r0 + A_VT_ROWS, :] = jnp.ones((BF16_ROWS, vat.shape[1]), BF16)

    qb = _head_norm_rope(z[:, OFF_QB:OFF_QB + B_WIDTH], ones_ref, gqb_ref[...], cb, sb,
                         HEAD_DIM // 2, Q_SCALE)
    kb = _head_norm_rope(z[:, OFF_KB:OFF_KB + B_WIDTH], ones_ref, gkb_ref[...], cb, sb,
                         HEAD_DIM // 2, 1.0)
    scratch = iter(perm_refs)
    for g, (q_ref, k_ref, v_ref) in enumerate(((qb0_ref, kb0_ref, vb0_ref),
                                               (qb1_ref, kb1_ref, vb1_ref),
                                               (qb2_ref, kb2_ref, vb2_ref))):
        dil = B_GROUP_CFG[g][1]
        c0 = OFF_VB + g * B_GROUP_WIDTH
        vb = [z[:, c0:c0 + LANES], z[:, c0 + LANES:c0 + 2 * LANES]]
        for out_ref, halves in ((q_ref, qb[2 * g:2 * g + 2]), (k_ref, kb[2 * g:2 * g + 2]),
                                (v_ref, vb)):
            if dil == 1:
                out_ref[0] = jnp.concatenate(halves, axis=1).astype(BF16)
            else:
                _store_residue_major(halves, next(scratch), out_ref, dil)
    uc_ref[0] = z[:, OFF_UC:OFF_UC + POOL_WIDTH]


def _proj(x, layer, gmix, w_main, ones_blk, gqa, gka, gqb, gkb, ca, sa, cb, sb):
    b, s, d = x.shape
    tm = PROJ_TM
    grid = (b, s // tm)
    row = lambda bi, i: (bi, i, 0)
    lay = lambda bi, i: (layer, 0, 0)
    tab = lambda bi, i: (i, 0)
    bsd = lambda w, dt: jax.ShapeDtypeStruct((b, s, w), dt)
    dils = [dil for _, dil in B_GROUP_CFG]
    out_shape = ([jax.ShapeDtypeStruct((b, A_Q_WIDTH, s), BF16), bsd(A_KV_WIDTH, BF16),
                  jax.ShapeDtypeStruct((b, A_KV_HEADS * A_VT_ROWS, s), BF16)]
                 + [jax.ShapeDtypeStruct((b, s // dil, dil * B_GROUP_WIDTH), BF16)
                    for dil in dils] * 3
                 + [bsd(POOL_WIDTH, F32)])
    out_specs = ([pl.BlockSpec((1, A_Q_WIDTH, tm), lambda bi, i: (bi, 0, i)),
                  pl.BlockSpec((1, tm, A_KV_WIDTH), row),
                  pl.BlockSpec((1, A_KV_HEADS * A_VT_ROWS, tm), lambda bi, i: (bi, 0, i))]
                 + [pl.BlockSpec((1, tm // dil, dil * B_GROUP_WIDTH), row) for dil in dils] * 3
                 + [pl.BlockSpec((1, tm, POOL_WIDTH), row)])
    n_perm = 3 * sum(dil > 1 for dil in dils)
    gain_spec = _resident((None, 1, LANES), lay)
    tab_spec = pl.BlockSpec((tm, LANES), tab)
    in_specs = [pl.BlockSpec((1, tm, d), row),
                _resident((None, 1, d), lay),
                _resident((None, d, MAIN_WIDTH), lay),
                _resident((MXU_DIM, MXU_DIM), lambda bi, i: (0, 0)),
                gain_spec, gain_spec, gain_spec, gain_spec,
                tab_spec, tab_spec, tab_spec, tab_spec]
    return pl.pallas_call(
        _proj_kernel, grid=grid, in_specs=in_specs, out_specs=out_specs, out_shape=out_shape,
        scratch_shapes=[pltpu.VMEM((2, tm, LANES), F32)] * n_perm,
        compiler_params=_params(2), name="proj",
    )(x, gmix, w_main, ones_blk, gqa, gka, gqb, gkb, ca, sa, cb, sb)


def _attn_a_kernel(qt_ref, k_ref, vt_ref, o_ref, s_ref, acc_ref, *, tq, tk):
    n_chunks = k_ref.shape[1] // tk
    nq = A_GROUP * tq
    for g in range(A_KV_HEADS):
        heads = [qt_ref[0, (g * A_GROUP + j) * HEAD_DIM:(g * A_GROUP + j + 1) * HEAD_DIM, :]
                 for j in range(A_GROUP)]
        qg = jnp.concatenate(heads, axis=1)
        zeros = jnp.zeros_like(qg)
        parts = [zeros] * A_KV_HEADS
        parts[g] = qg
        rhs = jnp.concatenate(parts, axis=0)

        def scores(c, slot, rhs=rhs):
            start = c * tk if isinstance(c, int) else pl.multiple_of(c * tk, tk)
            sc = _dot(k_ref[0, pl.ds(start, tk), :], rhs)
            s_ref[slot] = sc
            return jnp.max(sc, axis=0, keepdims=True)

        def absorb(c, slot, top, carry, g=g):
            m, acc = carry
            start = c * tk if isinstance(c, int) else pl.multiple_of(c * tk, tk)
            m_new = jnp.maximum(m, top)
            alpha = jnp.exp2(m - m_new)
            p = jnp.exp2(s_ref[slot] - m_new)
            vt = vt_ref[0, g * A_VT_ROWS:(g + 1) * A_VT_ROWS, pl.ds(start, tk)]
            return m_new, alpha * acc + _dot(vt, p.astype(BF16))

        def body(i, carry):
            m, acc, top0 = carry
            c = 2 * i
            top1 = scores(c + 1, 1)
            m, acc = absorb(c, 0, top0, (m, acc))
            top0 = scores(c + 2, 0)
            m, acc = absorb(c + 1, 1, top1, (m, acc))
            return m, acc, top0

        top0 = scores(0, 0)
        carry = (jnp.full((1, nq), NEG_INF, F32), jnp.zeros((A_VT_ROWS, nq), F32), top0)
        m, acc, top0 = lax.fori_loop(0, n_chunks // 2 - 1, body, carry)
        top1 = scores(n_chunks - 1, 1)
        carry = absorb(n_chunks - 2, 0, top0, (m, acc))
        _, acc = absorb(n_chunks - 1, 1, top1, carry)
        o = acc[0:HEAD_DIM] / acc[HEAD_DIM:HEAD_DIM + 1]
        for j in range(A_GROUP):
            hd = (g * A_GROUP + j) * HEAD_DIM
            acc_ref[hd:hd + HEAD_DIM, :] = o[:, j * tq:(j + 1) * tq]
    o_ref[0] = acc_ref[...].T.astype(BF16)


def _attn_a(qat, ka, vat):
    b, _, s = qat.shape
    tq, tk = ATTN_A_TQ, ATTN_A_TK
    assert (s // tk) % 2 == 0 and s // tk >= 2
    return pl.pallas_call(
        functools.partial(_attn_a_kernel, tq=tq, tk=tk),
        grid=(b, s // tq),
        in_specs=[pl.BlockSpec((1, A_Q_WIDTH, tq), lambda bi, i: (bi, 0, i)),
                  pl.BlockSpec((1, s, A_KV_WIDTH), lambda bi, i: (bi, 0, 0)),
                  pl.BlockSpec((1, A_KV_HEADS * A_VT_ROWS, s), lambda bi, i: (bi, 0, 0))],
        out_specs=pl.BlockSpec((1, tq, A_Q_WIDTH), lambda bi, i: (bi, i, 0)),
        out_shape=jax.ShapeDtypeStruct((b, s, A_Q_WIDTH), BF16),
        scratch_shapes=[pltpu.VMEM((2, tk, A_GROUP * tq), F32),
                        pltpu.VMEM((A_Q_WIDTH, tq), F32)],
        compiler_params=_params(2), name="attn_a",
    )(qat, ka, vat)


def _attn_b_kernel(q_ref, k_ref, v_ref, o_ref, lse_ref, *, tl, half_span):
    sub_len = k_ref.shape[1]
    qc = ATTN_B_QC
    win = qc + 2 * half_span
    lt = pl.program_id(2)
    for c in range(tl // qc):
        row0 = lt * tl + c * qc
        ws = pl.multiple_of(jnp.clip(row0 - half_span, 0, sub_len - win), half_span)
        kw = k_ref[0, pl.ds(ws, win), :]
        vw = v_ref[0, pl.ds(ws, win), :]
        qpos = row0 + lax.broadcasted_iota(jnp.int32, (qc, win), 0)
        kpos = ws + lax.broadcasted_iota(jnp.int32, (qc, win), 1)
        valid = jnp.abs(qpos - kpos) <= half_span
        outs, lses = [], []
        for hh in range(B_HEADS_PER_GROUP):
            cols = slice(hh * HEAD_DIM, (hh + 1) * HEAD_DIM)
            qh = q_ref[0, c * qc:(c + 1) * qc, cols]
            sc = lax.dot_general(qh, kw[:, cols], (((1,), (1,)), ((), ())),
                                 preferred_element_type=F32)
            sc = jnp.where(valid, sc, NEG_INF)
            m = jnp.max(sc, axis=-1, keepdims=True)
            p = jnp.exp2(sc - m)
            l = jnp.sum(p, axis=-1, keepdims=True)
            outs.append(_dot(p.astype(BF16), vw[:, cols]) / l)
            lses.append(jnp.broadcast_to(m + jnp.log2(l), (qc, HEAD_DIM)))
        o_ref[0, c * qc:(c + 1) * qc, :] = jnp.concatenate(outs, axis=1)
        lse_ref[0, c * qc:(c + 1) * qc, :] = jnp.concatenate(lses, axis=1)


def _attn_b(q, k, v, window, dilation):
    b, sub_len, _ = q.shape
    w = B_GROUP_WIDTH
    half_span = window // (2 * dilation)
    tl = min(ATTN_B_TL, sub_len)
    tile = pl.BlockSpec((1, tl, w), lambda bi, r, lt: (bi, lt, r))
    whole = pl.BlockSpec((1, sub_len, w), lambda bi, r, lt: (bi, 0, r))
    return pl.pallas_call(
        functools.partial(_attn_b_kernel, tl=tl, half_span=half_span),
        grid=(b, dilation, sub_len // tl),
        in_specs=[tile, whole, whole],
        out_specs=[tile, tile],
        out_shape=[jax.ShapeDtypeStruct((b, sub_len, dilation * w), F32)] * 2,
        compiler_params=_params(3), name=f"attn_b_d{dilation}",
    )(q, k, v)


def _merge_kernel(x_ref, oa_ref, ob0_ref, ob1_ref, ob2_ref, ls0_ref, ls1_ref, ls2_ref,
                  ucp_ref, uc_ref, ucn_ref,
                  gmix_ref, wg_ref, bg_ref, wa_ref, wb_ref, wc_ref, lin_ref, psc_ref, wo_ref,
                  out_ref, ext_ref, *perm_refs, tm, seq_len):
    i = pl.program_id(1)
    d = x_ref.shape[2]
    x = x_ref[0]
    h = _rms_norm_rows(x, gmix_ref[...]).astype(BF16)
    gz = _dot(h, wg_ref[...]) + bg_ref[...]
    gates = 1.0 / (1.0 + jnp.exp(-gz))

    ya = _dot(oa_ref[0], wa_ref[...])

    scratch = iter(perm_refs)
    token_major = lambda ref, dil: ref[0] if dil == 1 else _load_token_major(ref, next(scratch), dil)
    dils = [dil for _, dil in B_GROUP_CFG]
    obs = [token_major(ref, dil) for ref, dil in zip((ob0_ref, ob1_ref, ob2_ref), dils)]
    ls = [token_major(ref, dil) for ref, dil in zip((ls0_ref, ls1_ref, ls2_ref), dils)]
    top = jnp.maximum(jnp.maximum(ls[0], ls[1]), ls[2])
    wts = [jnp.exp2(t - top) for t in ls]
    ob = (wts[0] * obs[0] + wts[1] * obs[1] + wts[2] * obs[2]) / (wts[0] + wts[1] + wts[2])
    yb = _dot(ob.astype(BF16), wb_ref[...])

    halo = POOL_HALO
    ext_ref[0:halo, :] = jnp.where(i > 0, ucp_ref[0], 0.0)
    ext_ref[halo:halo + tm, :] = uc_ref[0]
    ext_ref[halo + tm:2 * halo + tm, :] = jnp.where(i < pl.num_programs(1) - 1, ucn_ref[0], 0.0)
    t = i * tm + lax.broadcasted_iota(jnp.int32, (tm, 1), 0)
    mixed = []
    for gi, w in enumerate(POOL_WINDOWS):
        cols = slice(gi * POOL_GROUP_WIDTH, (gi + 1) * POOL_GROUP_WIDTH)
        tot = None
        for j in range(w):
            r0 = halo - w // 2 + j
            part = ext_ref[r0:r0 + tm, cols]
            tot = part if tot is None else tot + part
        cnt = jnp.minimum(t + (w - w // 2), seq_len) - jnp.maximum(t - w // 2, 0)
        pooled = tot / cnt.astype(F32) - ext_ref[halo:halo + tm, cols]
        mixed.append(_dot(pooled.astype(BF16), lin_ref[gi]) * psc_ref[:, cols])
    yc = _dot(jnp.concatenate(mixed, axis=1).astype(BF16), wc_ref[...])

    merged = gates[:, 0:d] * ya + gates[:, d:2 * d] * yb + gates[:, 2 * d:3 * d] * yc
    out_ref[0] = x + _dot(merged.astype(BF16), wo_ref[...])


def _merge(x, layer, oa, obs, lss, uc, gmix, wg, bg, wa, wb, wc, lin, psc, wo):
    b, s, d = x.shape
    tm = MERGE_TM
    nt = s // tm
    hb = tm // POOL_HALO
    n_hblk = s // POOL_HALO
    row = lambda bi, i: (bi, i, 0)
    lay = lambda bi, i: (layer, 0, 0)
    tile = lambda w: pl.BlockSpec((1, tm, w), row)
    dils = [dil for _, dil in B_GROUP_CFG]
    b_tile = [pl.BlockSpec((1, tm // dil, dil * B_GROUP_WIDTH), row) for dil in dils]
    n_perm = 2 * sum(dil > 1 for dil in dils)
    in_specs = ([tile(d), tile(A_Q_WIDTH)] + b_tile * 2
                + [pl.BlockSpec((1, POOL_HALO, POOL_WIDTH),
                                lambda bi, i: (bi, jnp.maximum(i * hb - 1, 0), 0)),
                   tile(POOL_WIDTH),
                   pl.BlockSpec((1, POOL_HALO, POOL_WIDTH),
                                lambda bi, i: (bi, jnp.minimum((i + 1) * hb, n_hblk - 1), 0)),
                   _resident((None, 1, d), lay),
                   _resident((None, d, 3 * d), lay),
                   _resident((None, 1, 3 * d), lay),
                   _resident((None, A_Q_WIDTH, d), lay),
                   _resident((None, B_GROUP_WIDTH, d), lay),
                   _resident((None, POOL_WIDTH, d), lay),
                   _resident((None, len(POOL_WINDOWS), POOL_GROUP_WIDTH, POOL_GROUP_WIDTH),
                             lambda bi, i: (layer, 0, 0, 0)),
                   _resident((None, 1, POOL_WIDTH), lay),
                   _resident((None, d, d), lay)])
    return pl.pallas_call(
        functools.partial(_merge_kernel, tm=tm, seq_len=s),
        grid=(b, nt), in_specs=in_specs, out_specs=tile(d),
        out_shape=jax.ShapeDtypeStruct((b, s, d), F32),
        scratch_shapes=([pltpu.VMEM((tm + 2 * POOL_HALO, POOL_WIDTH), F32)]
                        + [pltpu.VMEM((2, tm, LANES), F32)] * n_perm),
        compiler_params=_params(2), name="merge",
    )(x, oa, *obs, *lss, uc, uc, uc, gmix, wg, bg, wa, wb, wc, lin, psc, wo)


def _ffn_kernel(x_ref, gn_ref, wgate_ref, wup_ref, wdown_ref, out_ref):
    x = x_ref[0]
    h = _rms_norm_rows(x, gn_ref[...]).astype(BF16)
    a = _dot(h, wgate_ref[...])
    u = _dot(h, wup_ref[...])
    act = (a / (1.0 + jnp.exp(-a))) * u
    out_ref[0] = x + _dot(act.astype(BF16), wdown_ref[...])


def _ffn(x, layer, gn, wgate, wup, wdown):
    b, s, d = x.shape
    dff = wgate.shape[2]
    tm = FFN_TM
    row = lambda bi, i: (bi, i, 0)
    lay = lambda bi, i: (layer, 0, 0)
    return pl.pallas_call(
        _ffn_kernel, grid=(b, s // tm),
        in_specs=[pl.BlockSpec((1, tm, d), row),
                  _resident((None, 1, d), lay),
                  _resident((None, d, dff), lay),
                  _resident((None, d, dff), lay),
                  _resident((None, dff, d), lay)],
        out_specs=pl.BlockSpec((1, tm, d), row),
        out_shape=jax.ShapeDtypeStruct((b, s, d), F32),
        compiler_params=_params(2), name="ffn",
    )(x, gn, wgate, wup, wdown)


def _rope_tables(s):
    def angles(pos, dim):
        inv = ROPE_THETA ** (-jnp.arange(0, dim, 2, dtype=F32) / dim)
        return pos.astype(F32)[:, None] * inv[None, :]

    t = jnp.arange(s)
    ang_row = angles(t // GRID_W, HEAD_DIM // 2)
    ang_col = angles(t % GRID_W, HEAD_DIM // 2)
    ang_seq = angles(t, HEAD_DIM)

    def table(parts):
        return jnp.tile(jnp.concatenate(parts, axis=1), (1, LANES // HEAD_DIM))

    ca = table([jnp.cos(ang_row)] * 2 + [jnp.cos(ang_col)] * 2)
    sa = table([-jnp.sin(ang_row), jnp.sin(ang_row), -jnp.sin(ang_col), jnp.sin(ang_col)])
    cb = table([jnp.cos(ang_seq)] * 2)
    sb = table([-jnp.sin(ang_seq), jnp.sin(ang_seq)])
    return ca, sa, cb, sb


def kernel(x, norm_mix, w_in, b_gate, qn_a, kn_a, qn_b, kn_b, pool_lin, pool_scale,
           w_branch_a, w_branch_b, w_branch_c, w_out, norm_ffn, w_ffn_gate, w_ffn_up, w_ffn_down):
    b, s, d = x.shape
    depth = w_in.shape[0]
    assert w_in.shape[2] == MAIN_WIDTH + 3 * d
    assert s % GRID_W == 0 and s % max(PROJ_TM, ATTN_A_TQ, ATTN_A_TK, MERGE_TM, FFN_TM) == 0
    for window, dil in B_GROUP_CFG:
        assert (s // dil) % ATTN_B_QC == 0 and s // dil >= ATTN_B_QC + window // dil

    ca, sa, cb, sb = _rope_tables(s)
    ones_blk = jnp.kron(jnp.eye(MXU_DIM // HEAD_DIM, dtype=F32),
                        jnp.ones((HEAD_DIM, HEAD_DIM), F32)).astype(BF16)
    row3 = lambda t: t.reshape(depth, 1, t.shape[-1])
    two_heads = lambda t: row3(jnp.tile(t, (1, LANES // HEAD_DIM)))
    w_main = w_in[:, :, :MAIN_WIDTH].astype(BF16)
    w_g = w_in[:, :, MAIN_WIDTH:].astype(BF16)
    gmix, gffn, bg, psc = row3(norm_mix), row3(norm_ffn), row3(b_gate), row3(pool_scale)
    gqa, gka, gqb, gkb = two_heads(qn_a), two_heads(kn_a), two_heads(qn_b), two_heads(kn_b)
    wa, wb, wc, wo = (t.astype(BF16) for t in (w_branch_a, w_branch_b, w_branch_c, w_out))
    lin = pool_lin.astype(BF16)
    wgate, wup, wdown = (t.astype(BF16) for t in (w_ffn_gate, w_ffn_up, w_ffn_down))

    for layer in range(depth):
        (qat, ka, vat, qb0, qb1, qb2, kb0, kb1, kb2, vb0, vb1, vb2, uc) = _proj(
            x, layer, gmix, w_main, ones_blk, gqa, gka, gqb, gkb, ca, sa, cb, sb)
        oa = _attn_a(qat, ka, vat)
        obs, lss = [], []
        for (window, dil), q, k, v in zip(B_GROUP_CFG, (qb0, qb1, qb2), (kb0, kb1, kb2),
                                          (vb0, vb1, vb2)):
            o, lse = _attn_b(q, k, v, window, dil)
            obs.append(o)
            lss.append(lse)
        x = _merge(x, layer, oa, obs, lss, uc, gmix, w_g, bg, wa, wb, wc, lin, psc, wo)
        x = _ffn(x, layer, gffn, wgate, wup, wdown)
    return x
```

```python
import functools
import math

import jax
import jax.numpy as jnp
from jax import lax
from jax.experimental import pallas as pl
from jax.experimental.pallas import tpu as pltpu

F32 = jnp.float32
BF16 = jnp.bfloat16

HEAD_DIM = 64
A_Q_HEADS = 8
A_KV_HEADS = 2
A_GROUP = A_Q_HEADS // A_KV_HEADS
B_GROUP_CFG = ((128, 1), (512, 4), (2048, 16))
B_HEADS_PER_GROUP = 4
B_GROUPS = len(B_GROUP_CFG)
POOL_WINDOWS = (2, 4, 8, 16)
POOL_GROUP_WIDTH = 128
GRID_W = 64
ROPE_THETA = 10000.0
EPS = 1e-6
NEG_INF = -1e30

A_Q_WIDTH = A_Q_HEADS * HEAD_DIM
A_KV_WIDTH = A_KV_HEADS * HEAD_DIM
B_GROUP_WIDTH = B_HEADS_PER_GROUP * HEAD_DIM
B_WIDTH = B_GROUPS * B_GROUP_WIDTH
POOL_WIDTH = len(POOL_WINDOWS) * POOL_GROUP_WIDTH
MAIN_WIDTH = A_Q_WIDTH + 2 * A_KV_WIDTH + 3 * B_WIDTH + POOL_WIDTH
OFF_QA = 0
OFF_KA = OFF_QA + A_Q_WIDTH
OFF_VA = OFF_KA + A_KV_WIDTH
OFF_QB = OFF_VA + A_KV_WIDTH
OFF_KB = OFF_QB + B_WIDTH
OFF_VB = OFF_KB + B_WIDTH
OFF_UC = OFF_VB + B_WIDTH

Q_SCALE = HEAD_DIM ** -0.5 * math.log2(math.e)

LANES = 128
SUBLANES = 8
BF16_ROWS = 16
MXU_DIM = 256
A_VT_ROWS = HEAD_DIM + BF16_ROWS
POOL_HALO = SUBLANES
VMEM_LIMIT_BYTES = 56 * 1024 * 1024

PROJ_TM = 512
ATTN_A_TQ = 256
ATTN_A_TK = 512
ATTN_A_SLOTS = 4
ATTN_A_AHEAD = 2
ATTN_B_TL = 512
ATTN_B_QC = 128
MERGE_TM = 512
FFN_TM = 512


def _params(n_axes):
    return pltpu.CompilerParams(dimension_semantics=("arbitrary",) * n_axes,
                                vmem_limit_bytes=VMEM_LIMIT_BYTES)


def _resident(block_shape, index_map):
    return pl.BlockSpec(block_shape, index_map, pipeline_mode=pl.Buffered(1))


def _rms_norm_rows(x, gain):
    ms = jnp.mean(x * x, axis=-1, keepdims=True)
    return (x * lax.rsqrt(ms + EPS)) * gain


def _dot(a, b):
    return jnp.dot(a, b, preferred_element_type=F32)


def _head_norm_rope(zs, ones_ref, gain, cos, sin, half, out_scale):
    tm, w = zs.shape
    lane = lax.broadcasted_iota(jnp.int32, (tm, LANES), 1)
    first = (lane & half) == 0
    pieces = []
    for c0 in range(0, w, MXU_DIM):
        cw = min(MXU_DIM, w - c0)
        zc = zs[:, c0:c0 + cw]
        z2 = zc * zc
        hi = z2.astype(BF16)
        lo = (z2 - hi.astype(F32)).astype(BF16)
        ones = ones_ref[:cw, :cw]
        ss = _dot(hi, ones) + _dot(lo, ones)
        r = lax.rsqrt(ss * (1.0 / HEAD_DIM) + EPS)
        for p0 in range(0, cw, LANES):
            xn = (zc[:, p0:p0 + LANES] * r[:, p0:p0 + LANES]) * gain
            partner = jnp.where(first, pltpu.roll(xn, LANES - half, 1), pltpu.roll(xn, half, 1))
            o = xn * cos + partner * sin
            if out_scale != 1.0:
                o = o * out_scale
            pieces.append(o)
    return pieces


def _store_residue_major(halves, scr_ref, out_ref, dil):
    n = halves[0].shape[0] // dil
    for hf, piece in enumerate(halves):
        scr_ref[hf] = piece
    for r in range(dil):
        for hf in range(2):
            c0 = r * B_GROUP_WIDTH + hf * LANES
            out_ref[0, :, c0:c0 + LANES] = scr_ref[hf, pl.ds(r, n, stride=dil), :].astype(BF16)


def _load_token_major(blk_ref, scr_ref, dil):
    n = blk_ref.shape[1]
    for r in range(dil):
        for hf in range(2):
            c0 = r * B_GROUP_WIDTH + hf * LANES
            scr_ref[hf, pl.ds(r, n, stride=dil), :] = blk_ref[0, :, c0:c0 + LANES]
    return jnp.concatenate([scr_ref[0], scr_ref[1]], axis=1)


def _proj_kernel(x_ref, gmix_ref, w_ref, ones_ref, gqa_ref, gka_ref, gqb_ref, gkb_ref,
                 ca_ref, sa_ref, cb_ref, sb_ref,
                 qat_ref, ka_ref, vat_ref,
                 qb0_ref, qb1_ref, qb2_ref, kb0_ref, kb1_ref, kb2_ref, vb0_ref, vb1_ref, vb2_ref,
                 uc_ref, *perm_refs):
    h = _rms_norm_rows(x_ref[0], gmix_ref[...]).astype(BF16)
    ca, sa, cb, sb = ca_ref[...], sa_ref[...], cb_ref[...], sb_ref[...]

    def section(c0, width):
        return _dot(h, w_ref[:, c0:c0 + width])

    qa = _head_norm_rope(section(OFF_QA, A_Q_WIDTH), ones_ref, gqa_ref[...], ca, sa,
                         HEAD_DIM // 4, Q_SCALE)
    for p, piece in enumerate(qa):
        qat_ref[0, p * LANES:(p + 1) * LANES, :] = piece.T.astype(BF16)
    zkv = section(OFF_KA, 2 * A_KV_WIDTH)
    ka = _head_norm_rope(zkv[:, 0:A_KV_WIDTH], ones_ref, gka_ref[...], ca, sa, HEAD_DIM // 4, 1.0)
    ka_ref[0] = ka[0].astype(BF16)
    vat = zkv[:, A_KV_WIDTH:2 * A_KV_WIDTH].T.astype(BF16)
    for g in range(A_KV_HEADS):
        r0 = g * A_VT_ROWS
        vat_ref[0, r0:r0 + HEAD_DIM, :] = vat[g * HEAD_DIM:(g + 1) * HEAD_DIM]
        vat_ref[0, r0 + HEAD_DIM:r0 + A_VT_ROWS, :] = jnp.ones((BF16_ROWS, vat.shape[1]), BF16)

    qb = _head_norm_rope(section(OFF_QB, B_WIDTH), ones_ref, gqb_ref[...], cb, sb,
                         HEAD_DIM // 2, Q_SCALE)
    kb = _head_norm_rope(section(OFF_KB, B_WIDTH), ones_ref, gkb_ref[...], cb, sb,
                         HEAD_DIM // 2, 1.0)
    zvb = section(OFF_VB, B_WIDTH)
    scratch = iter(perm_refs)
    for g, (q_ref, k_ref, v_ref) in enumerate(((qb0_ref, kb0_ref, vb0_ref),
                                               (qb1_ref, kb1_ref, vb1_ref),
                                               (qb2_ref, kb2_ref, vb2_ref))):
        dil = B_GROUP_CFG[g][1]
        c0 = g * B_GROUP_WIDTH
        vb = [zvb[:, c0:c0 + LANES], zvb[:, c0 + LANES:c0 + 2 * LANES]]
        for out_ref, halves in ((q_ref, qb[2 * g:2 * g + 2]), (k_ref, kb[2 * g:2 * g + 2]),
                                (v_ref, vb)):
            if dil == 1:
                out_ref[0] = jnp.concatenate(halves, axis=1).astype(BF16)
            else:
                _store_residue_major(halves, next(scratch), out_ref, dil)
    uc_ref[0] = section(OFF_UC, POOL_WIDTH)


def _proj(x, layer, gmix, w_main, ones_blk, gqa, gka, gqb, gkb, ca, sa, cb, sb):
    b, s, d = x.shape
    tm = PROJ_TM
    grid = (b, s // tm)
    row = lambda bi, i: (bi, i, 0)
    lay = lambda bi, i: (layer, 0, 0)
    tab = lambda bi, i: (i, 0)
    bsd = lambda w, dt: jax.ShapeDtypeStruct((b, s, w), dt)
    dils = [dil for _, dil in B_GROUP_CFG]
    out_shape = ([jax.ShapeDtypeStruct((b, A_Q_WIDTH, s), BF16), bsd(A_KV_WIDTH, BF16),
                  jax.ShapeDtypeStruct((b, A_KV_HEADS * A_VT_ROWS, s), BF16)]
                 + [jax.ShapeDtypeStruct((b, s // dil, dil * B_GROUP_WIDTH), BF16)
                    for dil in dils] * 3
                 + [bsd(POOL_WIDTH, F32)])
    out_specs = ([pl.BlockSpec((1, A_Q_WIDTH, tm), lambda bi, i: (bi, 0, i)),
                  pl.BlockSpec((1, tm, A_KV_WIDTH), row),
                  pl.BlockSpec((1, A_KV_HEADS * A_VT_ROWS, tm), lambda bi, i: (bi, 0, i))]
                 + [pl.BlockSpec((1, tm // dil, dil * B_GROUP_WIDTH), row) for dil in dils] * 3
                 + [pl.BlockSpec((1, tm, POOL_WIDTH), row)])
    n_perm = 3 * sum(dil > 1 for dil in dils)
    gain_spec = _resident((None, 1, LANES), lay)
    tab_spec = pl.BlockSpec((tm, LANES), tab)
    in_specs = [pl.BlockSpec((1, tm, d), row),
                _resident((None, 1, d), lay),
                _resident((None, d, MAIN_WIDTH), lay),
                _resident((MXU_DIM, MXU_DIM), lambda bi, i: (0, 0)),
                gain_spec, gain_spec, gain_spec, gain_spec,
                tab_spec, tab_spec, tab_spec, tab_spec]
    return pl.pallas_call(
        _proj_kernel, grid=grid, in_specs=in_specs, out_specs=out_specs, out_shape=out_shape,
        scratch_shapes=[pltpu.VMEM((2, tm, LANES), F32)] * n_perm,
        compiler_params=_params(2), name="proj",
    )(x, gmix, w_main, ones_blk, gqa, gka, gqb, gkb, ca, sa, cb, sb)


def _attn_a_kernel(qt_ref, k_ref, vt_ref, o_ref, s_ref, acc_ref, *, tq, tk, n_slots, ahead):
    n_chunks = k_ref.shape[1] // tk
    nq = A_GROUP * tq
    for g in range(A_KV_HEADS):
        heads = [qt_ref[0, (g * A_GROUP + j) * HEAD_DIM:(g * A_GROUP + j + 1) * HEAD_DIM, :]
                 for j in range(A_GROUP)]
        qg = jnp.concatenate(heads, axis=1)
        zeros = jnp.zeros_like(qg)
        parts = [zeros] * A_KV_HEADS
        parts[g] = qg
        rhs = jnp.concatenate(parts, axis=0)

        def scores(c, slot, rhs=rhs):
            start = c * tk if isinstance(c, int) else pl.multiple_of(c * tk, tk)
            sc = _dot(k_ref[0, pl.ds(start, tk), :], rhs)
            s_ref[slot] = sc
            return jnp.max(sc, axis=0, keepdims=True)

        def absorb(c, slot, top, carry, g=g):
            m, acc = carry
            start = c * tk if isinstance(c, int) else pl.multiple_of(c * tk, tk)
            m_new = jnp.maximum(m, top)
            alpha = jnp.exp2(m - m_new)
            p = jnp.exp2(s_ref[slot] - m_new)
            vt = vt_ref[0, g * A_VT_ROWS:(g + 1) * A_VT_ROWS, pl.ds(start, tk)]
            return m_new, alpha * acc + _dot(vt, p.astype(BF16))

        def step(c, pos, tops, state):
            nxt = c + ahead
            if not isinstance(nxt, int) or nxt < n_chunks:
                tops = tops + (scores(nxt, (pos + ahead) % n_slots),)
            state = absorb(c, pos % n_slots, tops[0], state)
            return tops[1:], state

        def body(i, carry):
            tops, state = carry[:ahead], carry[ahead:]
            for j in range(n_slots):
                tops, state = step(i * n_slots + j, j, tops, state)
            return tops + state

        tops = tuple(scores(c, c) for c in range(ahead))
        state = (jnp.full((1, nq), NEG_INF, F32), jnp.zeros((A_VT_ROWS, nq), F32))
        trips = (n_chunks - ahead) // n_slots
        carry = lax.fori_loop(0, trips, body, tops + state)
        tops, state = carry[:ahead], carry[ahead:]
        for c in range(trips * n_slots, n_chunks):
            tops, state = step(c, c, tops, state)
        acc = state[1]
        o = acc[0:HEAD_DIM] / acc[HEAD_DIM:HEAD_DIM + 1]
        for j in range(A_GROUP):
            hd = (g * A_GROUP + j) * HEAD_DIM
            acc_ref[hd:hd + HEAD_DIM, :] = o[:, j * tq:(j + 1) * tq]
    o_ref[0] = acc_ref[...].T.astype(BF16)


def _attn_a(qat, ka, vat):
    b, _, s = qat.shape
    tq, tk = ATTN_A_TQ, ATTN_A_TK
    n_slots, ahead = ATTN_A_SLOTS, ATTN_A_AHEAD
    assert ahead < n_slots and s // tk >= ahead
    return pl.pallas_call(
        functools.partial(_attn_a_kernel, tq=tq, tk=tk, n_slots=n_slots, ahead=ahead),
        grid=(b, s // tq),
        in_specs=[pl.BlockSpec((1, A_Q_WIDTH, tq), lambda bi, i: (bi, 0, i)),
                  pl.BlockSpec((1, s, A_KV_WIDTH), lambda bi, i: (bi, 0, 0)),
                  pl.BlockSpec((1, A_KV_HEADS * A_VT_ROWS, s), lambda bi, i: (bi, 0, 0))],
        out_specs=pl.BlockSpec((1, tq, A_Q_WIDTH), lambda bi, i: (bi, i, 0)),
        out_shape=jax.ShapeDtypeStruct((b, s, A_Q_WIDTH), BF16),
        scratch_shapes=[pltpu.VMEM((n_slots, tk, A_GROUP * tq), F32),
                        pltpu.VMEM((A_Q_WIDTH, tq), F32)],
        compiler_params=_params(2), name="attn_a",
    )(qat, ka, vat)


def _attn_b_kernel(q_ref, k_ref, v_ref, o_ref, lse_ref, *, tl, half_span):
    sub_len = k_ref.shape[1]
    qc = ATTN_B_QC
    win = qc + 2 * half_span
    lt = pl.program_id(2)
    lane_head = lax.broadcasted_iota(jnp.int32, (1, B_GROUP_WIDTH), 1) // HEAD_DIM
    head_lanes = [lane_head == hh for hh in range(B_HEADS_PER_GROUP)]
    head_ones = [jnp.where(hl, 1.0, 0.0).astype(BF16) for hl in head_lanes]
    for c in range(tl // qc):
        row0 = lt * tl + c * qc
        ws = pl.multiple_of(jnp.clip(row0 - half_span, 0, sub_len - win), half_span)
        q = q_ref[0, c * qc:(c + 1) * qc, :]
        kw = k_ref[0, pl.ds(ws, win), :]
        vw = v_ref[0, pl.ds(ws, win), :]
        qpos = row0 + lax.broadcasted_iota(jnp.int32, (qc, win), 0)
        kpos = ws + lax.broadcasted_iota(jnp.int32, (qc, win), 1)
        valid = jnp.abs(qpos - kpos) <= half_span
        k_stack = jnp.concatenate([kw * ones for ones in head_ones], axis=0)
        v_stack = jnp.concatenate(
            [jnp.concatenate([vw * ones, jnp.broadcast_to(ones, vw.shape)], axis=1)
             for ones in head_ones], axis=0)
        sc_all = lax.dot_general(q, k_stack, (((1,), (1,)), ((), ())),
                                 preferred_element_type=F32)
        m_all = None
        probs = []
        for hh in range(B_HEADS_PER_GROUP):
            sc = jnp.where(valid, sc_all[:, hh * win:(hh + 1) * win], NEG_INF)
            m = jnp.max(sc, axis=-1, keepdims=True)
            probs.append(jnp.exp2(sc - m).astype(BF16))
            m_all = jnp.where(head_lanes[hh], m, 0.0 if m_all is None else m_all)
        ol = _dot(jnp.concatenate(probs, axis=1), v_stack)
        o_ref[0, c * qc:(c + 1) * qc, :] = ol[:, 0:B_GROUP_WIDTH] / ol[:, B_GROUP_WIDTH:]
        lse_ref[0, c * qc:(c + 1) * qc, :] = m_all + jnp.log2(ol[:, B_GROUP_WIDTH:])


def _attn_b(q, k, v, window, dilation):
    b, sub_len, _ = q.shape
    w = B_GROUP_WIDTH
    half_span = window // (2 * dilation)
    tl = min(ATTN_B_TL, sub_len)
    tile = pl.BlockSpec((1, tl, w), lambda bi, r, lt: (bi, lt, r))
    whole = pl.BlockSpec((1, sub_len, w), lambda bi, r, lt: (bi, 0, r))
    return pl.pallas_call(
        functools.partial(_attn_b_kernel, tl=tl, half_span=half_span),
        grid=(b, dilation, sub_len // tl),
        in_specs=[tile, whole, whole],
        out_specs=[tile, tile],
        out_shape=[jax.ShapeDtypeStruct((b, sub_len, dilation * w), F32)] * 2,
        compiler_params=_params(3), name=f"attn_b_d{dilation}",
    )(q, k, v)


def _merge_kernel(x_ref, oa_ref, ob0_ref, ob1_ref, ob2_ref, ls0_ref, ls1_ref, ls2_ref,
                  ucp_ref, uc_ref, ucn_ref,
                  gmix_ref, wg_ref, bg_ref, wa_ref, wb_ref, wc_ref, lin_ref, psc_ref, wo_ref,
                  out_ref, ext_ref, *perm_refs, tm, seq_len):
    i = pl.program_id(1)
    d = x_ref.shape[2]
    x = x_ref[0]
    h = _rms_norm_rows(x, gmix_ref[...]).astype(BF16)

    scratch = iter(perm_refs)
    token_major = lambda ref, dil: ref[0] if dil == 1 else _load_token_major(ref, next(scratch), dil)
    dils = [dil for _, dil in B_GROUP_CFG]
    obs = [token_major(ref, dil) for ref, dil in zip((ob0_ref, ob1_ref, ob2_ref), dils)]
    ls = [token_major(ref, dil) for ref, dil in zip((ls0_ref, ls1_ref, ls2_ref), dils)]
    top = jnp.maximum(jnp.maximum(ls[0], ls[1]), ls[2])
    wts = [jnp.exp2(t - top) for t in ls]
    ob = (wts[0] * obs[0] + wts[1] * obs[1] + wts[2] * obs[2]) / (wts[0] + wts[1] + wts[2])

    halo = POOL_HALO
    ext_ref[0:halo, :] = jnp.where(i > 0, ucp_ref[0], 0.0)
    ext_ref[halo:halo + tm, :] = uc_ref[0]
    ext_ref[halo + tm:2 * halo + tm, :] = jnp.where(i < pl.num_programs(1) - 1, ucn_ref[0], 0.0)
    t = i * tm + lax.broadcasted_iota(jnp.int32, (tm, 1), 0)
    mixed = []
    for gi, w in enumerate(POOL_WINDOWS):
        cols = slice(gi * POOL_GROUP_WIDTH, (gi + 1) * POOL_GROUP_WIDTH)
        tot = None
        for j in range(w):
            r0 = halo - w // 2 + j
            part = ext_ref[r0:r0 + tm, cols]
            tot = part if tot is None else tot + part
        cnt = jnp.minimum(t + (w - w // 2), seq_len) - jnp.maximum(t - w // 2, 0)
        pooled = tot / cnt.astype(F32) - ext_ref[halo:halo + tm, cols]
        mixed.append(_dot(pooled.astype(BF16), lin_ref[gi]) * psc_ref[:, cols])

    ya = _dot(oa_ref[0], wa_ref[...])
    yb = _dot(ob.astype(BF16), wb_ref[...])
    yc = _dot(jnp.concatenate(mixed, axis=1).astype(BF16), wc_ref[...])
    gz = _dot(h, wg_ref[...]) + bg_ref[...]
    gates = 1.0 / (1.0 + jnp.exp(-gz))
    merged = gates[:, 0:d] * ya + gates[:, d:2 * d] * yb + gates[:, 2 * d:3 * d] * yc
    out_ref[0] = x + _dot(merged.astype(BF16), wo_ref[...])


def _merge(x, layer, oa, obs, lss, uc, gmix, wg, bg, wa, wb, wc, lin, psc, wo):
    b, s, d = x.shape
    tm = MERGE_TM
    nt = s // tm
    hb = tm // POOL_HALO
    n_hblk = s // POOL_HALO
    row = lambda bi, i: (bi, i, 0)
    lay = lambda bi, i: (layer, 0, 0)
    tile = lambda w: pl.BlockSpec((1, tm, w), row)
    dils = [dil for _, dil in B_GROUP_CFG]
    b_tile = [pl.BlockSpec((1, tm // dil, dil * B_GROUP_WIDTH), row) for dil in dils]
    n_perm = 2 * sum(dil > 1 for dil in dils)
    in_specs = ([tile(d), tile(A_Q_WIDTH)] + b_tile * 2
                + [pl.BlockSpec((1, POOL_HALO, POOL_WIDTH),
                                lambda bi, i: (bi, jnp.maximum(i * hb - 1, 0), 0)),
                   tile(POOL_WIDTH),
                   pl.BlockSpec((1, POOL_HALO, POOL_WIDTH),
                                lambda bi, i: (bi, jnp.minimum((i + 1) * hb, n_hblk - 1), 0)),
                   _resident((None, 1, d), lay),
                   _resident((None, d, 3 * d), lay),
                   _resident((None, 1, 3 * d), lay),
                   _resident((None, A_Q_WIDTH, d), lay),
                   _resident((None, B_GROUP_WIDTH, d), lay),
                   _resident((None, POOL_WIDTH, d), lay),
                   _resident((None, len(POOL_WINDOWS), POOL_GROUP_WIDTH, POOL_GROUP_WIDTH),
                             lambda bi, i: (layer, 0, 0, 0)),
                   _resident((None, 1, POOL_WIDTH), lay),
                   _resident((None, d, d), lay)])
    return pl.pallas_call(
        functools.partial(_merge_kernel, tm=tm, seq_len=s),
        grid=(b, nt), in_specs=in_specs, out_specs=tile(d),
        out_shape=jax.ShapeDtypeStruct((b, s, d), F32),
        scratch_shapes=([pltpu.VMEM((tm + 2 * POOL_HALO, POOL_WIDTH), F32)]
                        + [pltpu.VMEM((2, tm, LANES), F32)] * n_perm),
        compiler_params=_params(2), name="merge",
    )(x, oa, *obs, *lss, uc, uc, uc, gmix, wg, bg, wa, wb, wc, lin, psc, wo)


def _ffn_kernel(x_ref, gn_ref, wgate_ref, wup_ref, wdown_ref, out_ref):
    x = x_ref[0]
    h = _rms_norm_rows(x, gn_ref[...]).astype(BF16)
    a = _dot(h, wgate_ref[...])
    u = _dot(h, wup_ref[...])
    act = (a / (1.0 + jnp.exp(-a))) * u
    out_ref[0] = x + _dot(act.astype(BF16), wdown_ref[...])


def _ffn(x, layer, gn, wgate, wup, wdown):
    b, s, d = x.shape
    dff = wgate.shape[2]
    tm = FFN_TM
    row = lambda bi, i: (bi, i, 0)
    lay = lambda bi, i: (layer, 0, 0)
    return pl.pallas_call(
        _ffn_kernel, grid=(b, s // tm),
        in_specs=[pl.BlockSpec((1, tm, d), row),
                  _resident((None, 1, d), lay),
                  _resident((None, d, dff), lay),
                  _resident((None, d, dff), lay),
                  _resident((None, dff, d), lay)],
        out_specs=pl.BlockSpec((1, tm, d), row),
        out_shape=jax.ShapeDtypeStruct((b, s, d), F32),
        compiler_params=_params(2), name="ffn",
    )(x, gn, wgate, wup, wdown)


def _rope_tables(s):
    def angles(pos, dim):
        inv = ROPE_THETA ** (-jnp.arange(0, dim, 2, dtype=F32) / dim)
        return pos.astype(F32)[:, None] * inv[None, :]

    t = jnp.arange(s)
    ang_row = angles(t // GRID_W, HEAD_DIM // 2)
    ang_col = angles(t % GRID_W, HEAD_DIM // 2)
    ang_seq = angles(t, HEAD_DIM)

    def table(parts):
        return jnp.tile(jnp.concatenate(parts, axis=1), (1, LANES // HEAD_DIM))

    ca = table([jnp.cos(ang_row)] * 2 + [jnp.cos(ang_col)] * 2)
    sa = table([-jnp.sin(ang_row), jnp.sin(ang_row), -jnp.sin(ang_col), jnp.sin(ang_col)])
    cb = table([jnp.cos(ang_seq)] * 2)
    sb = table([-jnp.sin(ang_seq), jnp.sin(ang_seq)])
    return ca, sa, cb, sb


def kernel(x, norm_mix, w_in, b_gate, qn_a, kn_a, qn_b, kn_b, pool_lin, pool_scale,
           w_branch_a, w_branch_b, w_branch_c, w_out, norm_ffn, w_ffn_gate, w_ffn_up, w_ffn_down):
    b, s, d = x.shape
    depth = w_in.shape[0]
    assert w_in.shape[2] == MAIN_WIDTH + 3 * d
    assert s % GRID_W == 0 and s % max(PROJ_TM, ATTN_A_TQ, ATTN_A_TK, MERGE_TM, FFN_TM) == 0
    for window, dil in B_GROUP_CFG:
        assert (s // dil) % ATTN_B_QC == 0 and s // dil >= ATTN_B_QC + window // dil

    ca, sa, cb, sb = _rope_tables(s)
    ones_blk = jnp.kron(jnp.eye(MXU_DIM // HEAD_DIM, dtype=F32),
                        jnp.ones((HEAD_DIM, HEAD_DIM), F32)).astype(BF16)
    row3 = lambda t: t.reshape(depth, 1, t.shape[-1])
    two_heads = lambda t: row3(jnp.tile(t, (1, LANES // HEAD_DIM)))
    w_main = w_in[:, :, :MAIN_WIDTH].astype(BF16)
    w_g = w_in[:, :, MAIN_WIDTH:].astype(BF16)
    gmix, gffn, bg, psc = row3(norm_mix), row3(norm_ffn), row3(b_gate), row3(pool_scale)
    gqa, gka, gqb, gkb = two_heads(qn_a), two_heads(kn_a), two_heads(qn_b), two_heads(kn_b)
    wa, wb, wc, wo = (t.astype(BF16) for t in (w_branch_a, w_branch_b, w_branch_c, w_out))
    lin = pool_lin.astype(BF16)
    wgate, wup, wdown = (t.astype(BF16) for t in (w_ffn_gate, w_ffn_up, w_ffn_down))

    for layer in range(depth):
        (qat, ka, vat, qb0, qb1, qb2, kb0, kb1, kb2, vb0, vb1, vb2, uc) = _proj(
            x, layer, gmix, w_main, ones_blk, gqa, gka, gqb, gkb, ca, sa, cb, sb)
        oa = _attn_a(qat, ka, vat)
        obs, lss = [], []
        for (window, dil), q, k, v in zip(B_GROUP_CFG, (qb0, qb1, qb2), (kb0, kb1, kb2),
                                          (vb0, vb1, vb2)):
            o, lse = _attn_b(q, k, v, window, dil)
            obs.append(o)
            lss.append(lse)
        x = _merge(x, layer, oa, obs, lss, uc, gmix, w_g, bg, wa, wb, wc, lin, psc, wo)
        x = _ffn(x, layer, gffn, wgate, wup, wdown)
    return x
```

```python
import functools
import math

import jax
import jax.numpy as jnp
from jax import lax
from jax.experimental import pallas as pl
from jax.experimental.pallas import tpu as pltpu

F32 = jnp.float32
BF16 = jnp.bfloat16

HEAD_DIM = 64
A_Q_HEADS = 8
A_KV_HEADS = 2
A_GROUP = A_Q_HEADS // A_KV_HEADS
B_GROUP_CFG = ((128, 1), (512, 4), (2048, 16))
B_HEADS_PER_GROUP = 4
B_GROUPS = len(B_GROUP_CFG)
POOL_WINDOWS = (2, 4, 8, 16)
POOL_GROUP_WIDTH = 128
GRID_W = 64
ROPE_THETA = 10000.0
EPS = 1e-6
NEG_INF = -1e30

A_Q_WIDTH = A_Q_HEADS * HEAD_DIM
A_KV_WIDTH = A_KV_HEADS * HEAD_DIM
B_GROUP_WIDTH = B_HEADS_PER_GROUP * HEAD_DIM
B_WIDTH = B_GROUPS * B_GROUP_WIDTH
POOL_WIDTH = len(POOL_WINDOWS) * POOL_GROUP_WIDTH
MAIN_WIDTH = A_Q_WIDTH + 2 * A_KV_WIDTH + 3 * B_WIDTH + POOL_WIDTH
OFF_QA = 0
OFF_KA = OFF_QA + A_Q_WIDTH
OFF_VA = OFF_KA + A_KV_WIDTH
OFF_QB = OFF_VA + A_KV_WIDTH
OFF_KB = OFF_QB + B_WIDTH
OFF_VB = OFF_KB + B_WIDTH
OFF_UC = OFF_VB + B_WIDTH

Q_SCALE = HEAD_DIM ** -0.5 * math.log2(math.e)

LANES = 128
SUBLANES = 8
BF16_ROWS = 16
MXU_DIM = 256
A_ONES_ROWS = 64
A_VT_ROWS = HEAD_DIM + A_ONES_ROWS
POOL_HALO = SUBLANES
VMEM_LIMIT_BYTES = 56 * 1024 * 1024

PROJ_TM = 512
ATTN_A_TQ = 256
ATTN_A_TK = 512
ATTN_A_SLOTS = 4
ATTN_A_AHEAD = 2
ATTN_B_TL = 512
ATTN_B_QC = 128
MERGE_TM = 512
FFN_TM = 512


def _params(n_axes):
    return pltpu.CompilerParams(dimension_semantics=("arbitrary",) * n_axes,
                                vmem_limit_bytes=VMEM_LIMIT_BYTES)


def _resident(block_shape, index_map):
    return pl.BlockSpec(block_shape, index_map, pipeline_mode=pl.Buffered(1))


def _rms_norm_rows(x, gain):
    ms = jnp.mean(x * x, axis=-1, keepdims=True)
    return (x * lax.rsqrt(ms + EPS)) * gain


def _dot(a, b):
    return jnp.dot(a, b, preferred_element_type=F32)


def _head_norm_rope(zs, ones_ref, gain, cos, sin, half, out_scale):
    tm, w = zs.shape
    lane = lax.broadcasted_iota(jnp.int32, (tm, LANES), 1)
    first = (lane & half) == 0
    pieces = []
    for c0 in range(0, w, MXU_DIM):
        cw = min(MXU_DIM, w - c0)
        zc = zs[:, c0:c0 + cw]
        z2 = zc * zc
        hi = z2.astype(BF16)
        lo = (z2 - hi.astype(F32)).astype(BF16)
        ones = ones_ref[:cw, :cw]
        ss = _dot(hi, ones) + _dot(lo, ones)
        r = lax.rsqrt(ss * (1.0 / HEAD_DIM) + EPS)
        for p0 in range(0, cw, LANES):
            xn = (zc[:, p0:p0 + LANES] * r[:, p0:p0 + LANES]) * gain
            partner = jnp.where(first, pltpu.roll(xn, LANES - half, 1), pltpu.roll(xn, half, 1))
            o = xn * cos + partner * sin
            if out_scale != 1.0:
                o = o * out_scale
            pieces.append(o)
    return pieces


def _store_residue_major(halves, scr_ref, out_ref, dil):
    n = halves[0].shape[0] // dil
    for hf, piece in enumerate(halves):
        scr_ref[hf] = piece
    for r in range(dil):
        for hf in range(2):
            c0 = r * B_GROUP_WIDTH + hf * LANES
            out_ref[0, :, c0:c0 + LANES] = scr_ref[hf, pl.ds(r, n, stride=dil), :].astype(BF16)


def _load_token_major(blk_ref, scr_ref, dil):
    n = blk_ref.shape[1]
    for r in range(dil):
        for hf in range(2):
            c0 = r * B_GROUP_WIDTH + hf * LANES
            scr_ref[hf, pl.ds(r, n, stride=dil), :] = blk_ref[0, :, c0:c0 + LANES]
    return jnp.concatenate([scr_ref[0], scr_ref[1]], axis=1)


def _proj_kernel(x_ref, gmix_ref, w_ref, ones_ref, gqa_ref, gka_ref, gqb_ref, gkb_ref,
                 ca_ref, sa_ref, cb_ref, sb_ref,
                 qat_ref, ka_ref, vat_ref,
                 qb0_ref, qb1_ref, qb2_ref, kb0_ref, kb1_ref, kb2_ref, vb0_ref, vb1_ref, vb2_ref,
                 uc_ref, *perm_refs):
    h = _rms_norm_rows(x_ref[0], gmix_ref[...]).astype(BF16)
    ca, sa, cb, sb = ca_ref[...], sa_ref[...], cb_ref[...], sb_ref[...]

    def section(c0, width):
        return _dot(h, w_ref[:, c0:c0 + width])

    qa = _head_norm_rope(section(OFF_QA, A_Q_WIDTH), ones_ref, gqa_ref[...], ca, sa,
                         HEAD_DIM // 4, Q_SCALE)
    for p, piece in enumerate(qa):
        qat_ref[0, p * LANES:(p + 1) * LANES, :] = piece.T.astype(BF16)
    zkv = section(OFF_KA, 2 * A_KV_WIDTH)
    ka = _head_norm_rope(zkv[:, 0:A_KV_WIDTH], ones_ref, gka_ref[...], ca, sa, HEAD_DIM // 4, 1.0)
    ka_ref[0] = ka[0].astype(BF16)
    vat = zkv[:, A_KV_WIDTH:2 * A_KV_WIDTH].T.astype(BF16)
    for g in range(A_KV_HEADS):
        r0 = g * A_VT_ROWS
        vat_ref[0, r0:r0 + HEAD_DIM, :] = vat[g * HEAD_DIM:(g + 1) * HEAD_DIM]
        vat_ref[0, r0 + HEAD_DIM:r0 + A_VT_ROWS, :] = jnp.ones((A_ONES_ROWS, vat.shape[1]), BF16)

    qb = _head_norm_rope(section(OFF_QB, B_WIDTH), ones_ref, gqb_ref[...], cb, sb,
                         HEAD_DIM // 2, Q_SCALE)
    kb = _head_norm_rope(section(OFF_KB, B_WIDTH), ones_ref, gkb_ref[...], cb, sb,
                         HEAD_DIM // 2, 1.0)
    zvb = section(OFF_VB, B_WIDTH)
    scratch = iter(perm_refs)
    for g, (q_ref, k_ref, v_ref) in enumerate(((qb0_ref, kb0_ref, vb0_ref),
                                               (qb1_ref, kb1_ref, vb1_ref),
                                               (qb2_ref, kb2_ref, vb2_ref))):
        dil = B_GROUP_CFG[g][1]
        c0 = g * B_GROUP_WIDTH
        vb = [zvb[:, c0:c0 + LANES], zvb[:, c0 + LANES:c0 + 2 * LANES]]
        for out_ref, halves in ((q_ref, qb[2 * g:2 * g + 2]), (k_ref, kb[2 * g:2 * g + 2]),
                                (v_ref, vb)):
            if dil == 1:
                out_ref[0] = jnp.concatenate(halves, axis=1).astype(BF16)
            else:
                _store_residue_major(halves, next(scratch), out_ref, dil)
    uc_ref[0] = section(OFF_UC, POOL_WIDTH)


def _proj(x, layer, gmix, w_main, ones_blk, gqa, gka, gqb, gkb, ca, sa, cb, sb):
    b, s, d = x.shape
    tm = PROJ_TM
    grid = (b, s // tm)
    row = lambda bi, i: (bi, i, 0)
    lay = lambda bi, i: (layer, 0, 0)
    tab = lambda bi, i: (i, 0)
    bsd = lambda w, dt: jax.ShapeDtypeStruct((b, s, w), dt)
    dils = [dil for _, dil in B_GROUP_CFG]
    out_shape = ([jax.ShapeDtypeStruct((b, A_Q_WIDTH, s), BF16), bsd(A_KV_WIDTH, BF16),
                  jax.ShapeDtypeStruct((b, A_KV_HEADS * A_VT_ROWS, s), BF16)]
                 + [jax.ShapeDtypeStruct((b, s // dil, dil * B_GROUP_WIDTH), BF16)
                    for dil in dils] * 3
                 + [bsd(POOL_WIDTH, F32)])
    out_specs = ([pl.BlockSpec((1, A_Q_WIDTH, tm), lambda bi, i: (bi, 0, i)),
                  pl.BlockSpec((1, tm, A_KV_WIDTH), row),
                  pl.BlockSpec((1, A_KV_HEADS * A_VT_ROWS, tm), lambda bi, i: (bi, 0, i))]
                 + [pl.BlockSpec((1, tm // dil, dil * B_GROUP_WIDTH), row) for dil in dils] * 3
                 + [pl.BlockSpec((1, tm, POOL_WIDTH), row)])
    n_perm = 3 * sum(dil > 1 for dil in dils)
    gain_spec = _resident((None, 1, LANES), lay)
    tab_spec = pl.BlockSpec((tm, LANES), tab)
    in_specs = [pl.BlockSpec((1, tm, d), row),
                _resident((None, 1, d), lay),
                _resident((None, d, MAIN_WIDTH), lay),
                _resident((MXU_DIM, MXU_DIM), lambda bi, i: (0, 0)),
                gain_spec, gain_spec, gain_spec, gain_spec,
                tab_spec, tab_spec, tab_spec, tab_spec]
    return pl.pallas_call(
        _proj_kernel, grid=grid, in_specs=in_specs, out_specs=out_specs, out_shape=out_shape,
        scratch_shapes=[pltpu.VMEM((2, tm, LANES), F32)] * n_perm,
        compiler_params=_params(2), name="proj",
    )(x, gmix, w_main, ones_blk, gqa, gka, gqb, gkb, ca, sa, cb, sb)


def _attn_a_kernel(qt_ref, k_ref, vt_ref, o_ref, s_ref, acc_ref, *, tq, tk, n_slots, ahead):
    n_chunks = k_ref.shape[1] // tk
    nq = A_GROUP * tq
    for g in range(A_KV_HEADS):
        heads = [qt_ref[0, (g * A_GROUP + j) * HEAD_DIM:(g * A_GROUP + j + 1) * HEAD_DIM, :]
                 for j in range(A_GROUP)]
        qg = jnp.concatenate(heads, axis=1)
        zeros = jnp.zeros_like(qg)
        parts = [zeros] * A_KV_HEADS
        parts[g] = qg
        rhs = jnp.concatenate(parts, axis=0)

        def scores(c, slot, rhs=rhs):
            start = c * tk if isinstance(c, int) else pl.multiple_of(c * tk, tk)
            sc = _dot(k_ref[0, pl.ds(start, tk), :], rhs)
            s_ref[slot] = sc
            return jnp.max(sc, axis=0, keepdims=True)

        def absorb(c, slot, top, carry, g=g):
            m, acc = carry
            start = c * tk if isinstance(c, int) else pl.multiple_of(c * tk, tk)
            m_new = jnp.maximum(m, top)
            alpha = jnp.exp2(m - m_new)
            p = jnp.exp2(s_ref[slot] - m_new)
            vt = vt_ref[0, g * A_VT_ROWS:(g + 1) * A_VT_ROWS, pl.ds(start, tk)]
            return m_new, alpha * acc + _dot(vt, p.astype(BF16))

        def step(c, pos, tops, state):
            nxt = c + ahead
            if not isinstance(nxt, int) or nxt < n_chunks:
                tops = tops + (scores(nxt, (pos + ahead) % n_slots),)
            state = absorb(c, pos % n_slots, tops[0], state)
            return tops[1:], state

        def body(i, carry):
            tops, state = carry[:ahead], carry[ahead:]
            for j in range(n_slots):
                tops, state = step(i * n_slots + j, j, tops, state)
            return tops + state

        tops = tuple(scores(c, c) for c in range(ahead))
        state = (jnp.full((1, nq), NEG_INF, F32), jnp.zeros((A_VT_ROWS, nq), F32))
        trips = (n_chunks - ahead) // n_slots
        carry = lax.fori_loop(0, trips, body, tops + state)
        tops, state = carry[:ahead], carry[ahead:]
        for c in range(trips * n_slots, n_chunks):
            tops, state = step(c, c, tops, state)
        acc = state[1]
        o = acc[0:HEAD_DIM] / acc[HEAD_DIM:HEAD_DIM + 1]
        for j in range(A_GROUP):
            hd = (g * A_GROUP + j) * HEAD_DIM
            acc_ref[hd:hd + HEAD_DIM, :] = o[:, j * tq:(j + 1) * tq]
    o_ref[0] = acc_ref[...].T.astype(BF16)


def _attn_a(qat, ka, vat):
    b, _, s = qat.shape
    tq, tk = ATTN_A_TQ, ATTN_A_TK
    n_slots, ahead = ATTN_A_SLOTS, ATTN_A_AHEAD
    assert ahead < n_slots and s // tk >= ahead
    return pl.pallas_call(
        functools.partial(_attn_a_kernel, tq=tq, tk=tk, n_slots=n_slots, ahead=ahead),
        grid=(b, s // tq),
        in_specs=[pl.BlockSpec((1, A_Q_WIDTH, tq), lambda bi, i: (bi, 0, i)),
                  pl.BlockSpec((1, s, A_KV_WIDTH), lambda bi, i: (bi, 0, 0)),
                  pl.BlockSpec((1, A_KV_HEADS * A_VT_ROWS, s), lambda bi, i: (bi, 0, 0))],
        out_specs=pl.BlockSpec((1, tq, A_Q_WIDTH), lambda bi, i: (bi, i, 0)),
        out_shape=jax.ShapeDtypeStruct((b, s, A_Q_WIDTH), BF16),
        scratch_shapes=[pltpu.VMEM((n_slots, tk, A_GROUP * tq), F32),
                        pltpu.VMEM((A_Q_WIDTH, tq), F32)],
        compiler_params=_params(2), name="attn_a",
    )(qat, ka, vat)


def _attn_b_kernel(q_ref, k_ref, v_ref, o_ref, lse_ref, *, tl, half_span):
    sub_len = k_ref.shape[1]
    qc = ATTN_B_QC
    win = qc + 2 * half_span
    lt = pl.program_id(2)
    lane_head = lax.broadcasted_iota(jnp.int32, (1, B_GROUP_WIDTH), 1) // HEAD_DIM
    head_lanes = [lane_head == hh for hh in range(B_HEADS_PER_GROUP)]
    head_ones = [jnp.where(hl, 1.0, 0.0).astype(BF16) for hl in head_lanes]
    for c in range(tl // qc):
        row0 = lt * tl + c * qc
        ws = pl.multiple_of(jnp.clip(row0 - half_span, 0, sub_len - win), half_span)
        q = q_ref[0, c * qc:(c + 1) * qc, :]
        kw = k_ref[0, pl.ds(ws, win), :]
        vw = v_ref[0, pl.ds(ws, win), :]
        qpos = row0 + lax.broadcasted_iota(jnp.int32, (qc, win), 0)
        kpos = ws + lax.broadcasted_iota(jnp.int32, (qc, win), 1)
        valid = jnp.abs(qpos - kpos) <= half_span
        k_stack = jnp.concatenate([kw * ones for ones in head_ones], axis=0)
        v_stack = jnp.concatenate(
            [jnp.concatenate([vw * ones, jnp.broadcast_to(ones, vw.shape)], axis=1)
             for ones in head_ones], axis=0)
        sc_all = lax.dot_general(q, k_stack, (((1,), (1,)), ((), ())),
                                 preferred_element_type=F32)
        m_all = None
        probs = []
        for hh in range(B_HEADS_PER_GROUP):
            sc = jnp.where(valid, sc_all[:, hh * win:(hh + 1) * win], NEG_INF)
            m = jnp.max(sc, axis=-1, keepdims=True)
            probs.append(jnp.exp2(sc - m).astype(BF16))
            m_all = jnp.where(head_lanes[hh], m, 0.0 if m_all is None else m_all)
        ol = _dot(jnp.concatenate(probs, axis=1), v_stack)
        o_ref[0, c * qc:(c + 1) * qc, :] = ol[:, 0:B_GROUP_WIDTH] / ol[:, B_GROUP_WIDTH:]
        lse_ref[0, c * qc:(c + 1) * qc, :] = m_all + jnp.log2(ol[:, B_GROUP_WIDTH:])


def _attn_b(q, k, v, window, dilation):
    b, sub_len, _ = q.shape
    w = B_GROUP_WIDTH
    half_span = window // (2 * dilation)
    tl = min(ATTN_B_TL, sub_len)
    tile = pl.BlockSpec((1, tl, w), lambda bi, r, lt: (bi, lt, r))
    whole = pl.BlockSpec((1, sub_len, w), lambda bi, r, lt: (bi, 0, r))
    return pl.pallas_call(
        functools.partial(_attn_b_kernel, tl=tl, half_span=half_span),
        grid=(b, dilation, sub_len // tl),
        in_specs=[tile, whole, whole],
        out_specs=[tile, tile],
        out_shape=[jax.ShapeDtypeStruct((b, sub_len, dilation * w), F32)] * 2,
        compiler_params=_params(3), name=f"attn_b_d{dilation}",
    )(q, k, v)


def _merge_kernel(x_ref, oa_ref, ob0_ref, ob1_ref, ob2_ref, ls0_ref, ls1_ref, ls2_ref,
                  ucp_ref, uc_ref, ucn_ref,
                  gmix_ref, wg_ref, bg_ref, wa_ref, wb_ref, wc_ref, lin_ref, psc_ref, wo_ref,
                  out_ref, ext_ref, *perm_refs, tm, seq_len):
    i = pl.program_id(1)
    d = x_ref.shape[2]
    x = x_ref[0]
    h = _rms_norm_rows(x, gmix_ref[...]).astype(BF16)

    scratch = iter(perm_refs)
    token_major = lambda ref, dil: ref[0] if dil == 1 else _load_token_major(ref, next(scratch), dil)
    dils = [dil for _, dil in B_GROUP_CFG]
    obs = [token_major(ref, dil) for ref, dil in zip((ob0_ref, ob1_ref, ob2_ref), dils)]
    ls = [token_major(ref, dil) for ref, dil in zip((ls0_ref, ls1_ref, ls2_ref), dils)]
    top = jnp.maximum(jnp.maximum(ls[0], ls[1]), ls[2])
    wts = [jnp.exp2(t - top) for t in ls]
    ob = (wts[0] * obs[0] + wts[1] * obs[1] + wts[2] * obs[2]) / (wts[0] + wts[1] + wts[2])

    halo = POOL_HALO
    ext_ref[0:halo, :] = jnp.where(i > 0, ucp_ref[0], 0.0)
    ext_ref[halo:halo + tm, :] = uc_ref[0]
    ext_ref[halo + tm:2 * halo + tm, :] = jnp.where(i < pl.num_programs(1) - 1, ucn_ref[0], 0.0)
    t = i * tm + lax.broadcasted_iota(jnp.int32, (tm, 1), 0)
    mixed = []
    for gi, w in enumerate(POOL_WINDOWS):
        cols = slice(gi * POOL_GROUP_WIDTH, (gi + 1) * POOL_GROUP_WIDTH)
        tot = None
        for j in range(w):
            r0 = halo - w // 2 + j
            part = ext_ref[r0:r0 + tm, cols]
            tot = part if tot is None else tot + part
        cnt = jnp.minimum(t + (w - w // 2), seq_len) - jnp.maximum(t - w // 2, 0)
        pooled = tot / cnt.astype(F32) - ext_ref[halo:halo + tm, cols]
        mixed.append(_dot(pooled.astype(BF16), lin_ref[gi]) * psc_ref[:, cols])

    ya = _dot(oa_ref[0], wa_ref[...])
    yb = _dot(ob.astype(BF16), wb_ref[...])
    yc = _dot(jnp.concatenate(mixed, axis=1).astype(BF16), wc_ref[...])
    gz = _dot(h, wg_ref[...]) + bg_ref[...]
    gates = 1.0 / (1.0 + jnp.exp(-gz))
    merged = gates[:, 0:d] * ya + gates[:, d:2 * d] * yb + gates[:, 2 * d:3 * d] * yc
    out_ref[0] = x + _dot(merged.astype(BF16), wo_ref[...])


def _merge(x, layer, oa, obs, lss, uc, gmix, wg, bg, wa, wb, wc, lin, psc, wo):
    b, s, d = x.shape
    tm = MERGE_TM
    nt = s // tm
    hb = tm // POOL_HALO
    n_hblk = s // POOL_HALO
    row = lambda bi, i: (bi, i, 0)
    lay = lambda bi, i: (layer, 0, 0)
    tile = lambda w: pl.BlockSpec((1, tm, w), row)
    dils = [dil for _, dil in B_GROUP_CFG]
    b_tile = [pl.BlockSpec((1, tm // dil, dil * B_GROUP_WIDTH), row) for dil in dils]
    n_perm = 2 * sum(dil > 1 for dil in dils)
    in_specs = ([tile(d), tile(A_Q_WIDTH)] + b_tile * 2
                + [pl.BlockSpec((1, POOL_HALO, POOL_WIDTH),
                                lambda bi, i: (bi, jnp.maximum(i * hb - 1, 0), 0)),
                   tile(POOL_WIDTH),
                   pl.BlockSpec((1, POOL_HALO, POOL_WIDTH),
                                lambda bi, i: (bi, jnp.minimum((i + 1) * hb, n_hblk - 1), 0)),
                   _resident((None, 1, d), lay),
                   _resident((None, d, 3 * d), lay),
                   _resident((None, 1, 3 * d), lay),
                   _resident((None, A_Q_WIDTH, d), lay),
                   _resident((None, B_GROUP_WIDTH, d), lay),
                   _resident((None, POOL_WIDTH, d), lay),
                   _resident((None, len(POOL_WINDOWS), POOL_GROUP_WIDTH, POOL_GROUP_WIDTH),
                             lambda bi, i: (layer, 0, 0, 0)),
                   _resident((None, 1, POOL_WIDTH), lay),
                   _resident((None, d, d), lay)])
    return pl.pallas_call(
        functools.partial(_merge_kernel, tm=tm, seq_len=s),
        grid=(b, nt), in_specs=in_specs, out_specs=tile(d),
        out_shape=jax.ShapeDtypeStruct((b, s, d), F32),
        scratch_shapes=([pltpu.VMEM((tm + 2 * POOL_HALO, POOL_WIDTH), F32)]
                        + [pltpu.VMEM((2, tm, LANES), F32)] * n_perm),
        compiler_params=_params(2), name="merge",
    )(x, oa, *obs, *lss, uc, uc, uc, gmix, wg, bg, wa, wb, wc, lin, psc, wo)


def _ffn_kernel(x_ref, gn_ref, wgate_ref, wup_ref, wdown_ref, out_ref):
    x = x_ref[0]
    h = _rms_norm_rows(x, gn_ref[...]).astype(BF16)
    a = _dot(h, wgate_ref[...])
    u = _dot(h, wup_ref[...])
    act = (a / (1.0 + jnp.exp(-a))) * u
    out_ref[0] = x + _dot(act.astype(BF16), wdown_ref[...])


def _ffn(x, layer, gn, wgate, wup, wdown):
    b, s, d = x.shape
    dff = wgate.shape[2]
    tm = FFN_TM
    row = lambda bi, i: (bi, i, 0)
    lay = lambda bi, i: (layer, 0, 0)
    return pl.pallas_call(
        _ffn_kernel, grid=(b, s // tm),
        in_specs=[pl.BlockSpec((1, tm, d), row),
                  _resident((None, 1, d), lay),
                  _resident((None, d, dff), lay),
                  _resident((None, d, dff), lay),
                  _resident((None, dff, d), lay)],
        out_specs=pl.BlockSpec((1, tm, d), row),
        out_shape=jax.ShapeDtypeStruct((b, s, d), F32),
        compiler_params=_params(2), name="ffn",
    )(x, gn, wgate, wup, wdown)


def _rope_tables(s):
    def angles(pos, dim):
        inv = ROPE_THETA ** (-jnp.arange(0, dim, 2, dtype=F32) / dim)
        return pos.astype(F32)[:, None] * inv[None, :]

    t = jnp.arange(s)
    ang_row = angles(t // GRID_W, HEAD_DIM // 2)
    ang_col = angles(t % GRID_W, HEAD_DIM // 2)
    ang_seq = angles(t, HEAD_DIM)

    def table(parts):
        return jnp.tile(jnp.concatenate(parts, axis=1), (1, LANES // HEAD_DIM))

    ca = table([jnp.cos(ang_row)] * 2 + [jnp.cos(ang_col)] * 2)
    sa = table([-jnp.sin(ang_row), jnp.sin(ang_row), -jnp.sin(ang_col), jnp.sin(ang_col)])
    cb = table([jnp.cos(ang_seq)] * 2)
    sb = table([-jnp.sin(ang_seq), jnp.sin(ang_seq)])
    return ca, sa, cb, sb


def kernel(x, norm_mix, w_in, b_gate, qn_a, kn_a, qn_b, kn_b, pool_lin, pool_scale,
           w_branch_a, w_branch_b, w_branch_c, w_out, norm_ffn, w_ffn_gate, w_ffn_up, w_ffn_down):
    b, s, d = x.shape
    depth = w_in.shape[0]
    assert w_in.shape[2] == MAIN_WIDTH + 3 * d
    assert s % GRID_W == 0 and s % max(PROJ_TM, ATTN_A_TQ, ATTN_A_TK, MERGE_TM, FFN_TM) == 0
    for window, dil in B_GROUP_CFG:
        assert (s // dil) % ATTN_B_QC == 0 and s // dil >= ATTN_B_QC + window // dil

    ca, sa, cb, sb = _rope_tables(s)
    ones_blk = jnp.kron(jnp.eye(MXU_DIM // HEAD_DIM, dtype=F32),
                        jnp.ones((HEAD_DIM, HEAD_DIM), F32)).astype(BF16)
    row3 = lambda t: t.reshape(depth, 1, t.shape[-1])
    two_heads = lambda t: row3(jnp.tile(t, (1, LANES // HEAD_DIM)))
    w_main = w_in[:, :, :MAIN_WIDTH].astype(BF16)
    w_g = w_in[:, :, MAIN_WIDTH:].astype(BF16)
    gmix, gffn, bg, psc = row3(norm_mix), row3(norm_ffn), row3(b_gate), row3(pool_scale)
    gqa, gka, gqb, gkb = two_heads(qn_a), two_heads(kn_a), two_heads(qn_b), two_heads(kn_b)
    wa, wb, wc, wo = (t.astype(BF16) for t in (w_branch_a, w_branch_b, w_branch_c, w_out))
    lin = pool_lin.astype(BF16)
    wgate, wup, wdown = (t.astype(BF16) for t in (w_ffn_gate, w_ffn_up, w_ffn_down))

    for layer in range(depth):
        (qat, ka, vat, qb0, qb1, qb2, kb0, kb1, kb2, vb0, vb1, vb2, uc) = _proj(
            x, layer, gmix, w_main, ones_blk, gqa, gka, gqb, gkb, ca, sa, cb, sb)
        oa = _attn_a(qat, ka, vat)
        obs, lss = [], []
        for (window, dil), q, k, v in zip(B_GROUP_CFG, (qb0, qb1, qb2), (kb0, kb1, kb2),
                                          (vb0, vb1, vb2)):
            o, lse = _attn_b(q, k, v, window, dil)
            obs.append(o)
            lss.append(lse)
        x = _merge(x, layer, oa, obs, lss, uc, gmix, w_g, bg, wa, wb, wc, lin, psc, wo)
        x = _ffn(x, layer, gffn, wgate, wup, wdown)
    return x
```

```python
import functools
import math

import jax
import jax.numpy as jnp
from jax import lax
from jax.experimental import pallas as pl
from jax.experimental.pallas import tpu as pltpu

F32 = jnp.float32
BF16 = jnp.bfloat16

HEAD_DIM = 64
A_Q_HEADS = 8
A_KV_HEADS = 2
A_GROUP = A_Q_HEADS // A_KV_HEADS
B_GROUP_CFG = ((128, 1), (512, 4), (2048, 16))
B_HEADS_PER_GROUP = 4
B_GROUPS = len(B_GROUP_CFG)
POOL_WINDOWS = (2, 4, 8, 16)
POOL_GROUP_WIDTH = 128
GRID_W = 64
ROPE_THETA = 10000.0
EPS = 1e-6
NEG_INF = -1e30

A_Q_WIDTH = A_Q_HEADS * HEAD_DIM
A_KV_WIDTH = A_KV_HEADS * HEAD_DIM
B_GROUP_WIDTH = B_HEADS_PER_GROUP * HEAD_DIM
B_WIDTH = B_GROUPS * B_GROUP_WIDTH
POOL_WIDTH = len(POOL_WINDOWS) * POOL_GROUP_WIDTH
MAIN_WIDTH = A_Q_WIDTH + 2 * A_KV_WIDTH + 3 * B_WIDTH + POOL_WIDTH
OFF_QA = 0
OFF_KA = OFF_QA + A_Q_WIDTH
OFF_VA = OFF_KA + A_KV_WIDTH
OFF_QB = OFF_VA + A_KV_WIDTH
OFF_KB = OFF_QB + B_WIDTH
OFF_VB = OFF_KB + B_WIDTH
OFF_UC = OFF_VB + B_WIDTH

Q_SCALE = HEAD_DIM ** -0.5 * math.log2(math.e)

LANES = 128
SUBLANES = 8
BF16_ROWS = 16
MXU_DIM = 256
A_ONES_ROWS = 64
A_VT_ROWS = HEAD_DIM + A_ONES_ROWS
POOL_HALO = SUBLANES
VMEM_LIMIT_BYTES = 56 * 1024 * 1024

PROJ_TM = 512
ATTN_A_TQ = 256
ATTN_A_TK = 512
ATTN_A_SLOTS = 4
ATTN_A_AHEAD = 2
ATTN_B_TL = 512
ATTN_B_QC = 128
MERGE_TM = 512
FFN_TM = 512


def _params(n_axes):
    return pltpu.CompilerParams(dimension_semantics=("arbitrary",) * n_axes,
                                vmem_limit_bytes=VMEM_LIMIT_BYTES)


def _resident(block_shape, index_map):
    return pl.BlockSpec(block_shape, index_map, pipeline_mode=pl.Buffered(1))


def _rms_norm_rows(x, gain):
    ms = jnp.mean(x * x, axis=-1, keepdims=True)
    return (x * lax.rsqrt(ms + EPS)) * gain


def _dot(a, b):
    return jnp.dot(a, b, preferred_element_type=F32)


def _head_norm_rope(zs, ones_ref, gain, cos, sin, half, out_scale):
    tm, w = zs.shape
    lane = lax.broadcasted_iota(jnp.int32, (tm, LANES), 1)
    first = (lane & half) == 0
    pieces = []
    for c0 in range(0, w, MXU_DIM):
        cw = min(MXU_DIM, w - c0)
        zc = zs[:, c0:c0 + cw]
        z2 = zc * zc
        hi = z2.astype(BF16)
        lo = (z2 - hi.astype(F32)).astype(BF16)
        ones = ones_ref[:cw, :cw]
        ss = _dot(hi, ones) + _dot(lo, ones)
        r = lax.rsqrt(ss * (1.0 / HEAD_DIM) + EPS)
        for p0 in range(0, cw, LANES):
            xn = (zc[:, p0:p0 + LANES] * r[:, p0:p0 + LANES]) * gain
            partner = jnp.where(first, pltpu.roll(xn, LANES - half, 1), pltpu.roll(xn, half, 1))
            o = xn * cos + partner * sin
            if out_scale != 1.0:
                o = o * out_scale
            pieces.append(o)
    return pieces


def _store_residue_major(halves, scr_ref, out_ref, dil):
    n = halves[0].shape[0] // dil
    for hf, piece in enumerate(halves):
        scr_ref[hf] = piece
    for r in range(dil):
        for hf in range(2):
            c0 = r * B_GROUP_WIDTH + hf * LANES
            out_ref[0, :, c0:c0 + LANES] = scr_ref[hf, pl.ds(r, n, stride=dil), :].astype(BF16)


def _load_token_major(blk_ref, scr_ref, dil):
    n = blk_ref.shape[1]
    for r in range(dil):
        for hf in range(2):
            c0 = r * B_GROUP_WIDTH + hf * LANES
            scr_ref[hf, pl.ds(r, n, stride=dil), :] = blk_ref[0, :, c0:c0 + LANES]
    return jnp.concatenate([scr_ref[0], scr_ref[1]], axis=1)


def _proj_kernel(x_ref, gmix_ref, w_ref, ones_ref, gqa_ref, gka_ref, gqb_ref, gkb_ref,
                 ca_ref, sa_ref, cb_ref, sb_ref,
                 qat_ref, ka_ref, vat_ref,
                 qb0_ref, qb1_ref, qb2_ref, kb0_ref, kb1_ref, kb2_ref, vb0_ref, vb1_ref, vb2_ref,
                 uc_ref, *perm_refs):
    h = _rms_norm_rows(x_ref[0], gmix_ref[...]).astype(BF16)
    ca, sa, cb, sb = ca_ref[...], sa_ref[...], cb_ref[...], sb_ref[...]

    def section(c0, width):
        return _dot(h, w_ref[:, c0:c0 + width])

    qa = _head_norm_rope(section(OFF_QA, A_Q_WIDTH), ones_ref, gqa_ref[...], ca, sa,
                         HEAD_DIM // 4, Q_SCALE)
    for p, piece in enumerate(qa):
        qat_ref[0, p * LANES:(p + 1) * LANES, :] = piece.T.astype(BF16)
    zkv = section(OFF_KA, 2 * A_KV_WIDTH)
    ka = _head_norm_rope(zkv[:, 0:A_KV_WIDTH], ones_ref, gka_ref[...], ca, sa, HEAD_DIM // 4, 1.0)
    ka_ref[0] = ka[0].astype(BF16)
    vat = zkv[:, A_KV_WIDTH:2 * A_KV_WIDTH].T.astype(BF16)
    for g in range(A_KV_HEADS):
        r0 = g * A_VT_ROWS
        vat_ref[0, r0:r0 + HEAD_DIM, :] = vat[g * HEAD_DIM:(g + 1) * HEAD_DIM]
        vat_ref[0, r0 + HEAD_DIM:r0 + A_VT_ROWS, :] = jnp.ones((A_ONES_ROWS, vat.shape[1]), BF16)

    qb = _head_norm_rope(section(OFF_QB, B_WIDTH), ones_ref, gqb_ref[...], cb, sb,
                         HEAD_DIM // 2, Q_SCALE)
    kb = _head_norm_rope(section(OFF_KB, B_WIDTH), ones_ref, gkb_ref[...], cb, sb,
                         HEAD_DIM // 2, 1.0)
    zvb = section(OFF_VB, B_WIDTH)
    scratch = iter(perm_refs)
    for g, (q_ref, k_ref, v_ref) in enumerate(((qb0_ref, kb0_ref, vb0_ref),
                                               (qb1_ref, kb1_ref, vb1_ref),
                                               (qb2_ref, kb2_ref, vb2_ref))):
        dil = B_GROUP_CFG[g][1]
        c0 = g * B_GROUP_WIDTH
        vb = [zvb[:, c0:c0 + LANES], zvb[:, c0 + LANES:c0 + 2 * LANES]]
        for out_ref, halves in ((q_ref, qb[2 * g:2 * g + 2]), (k_ref, kb[2 * g:2 * g + 2]),
                                (v_ref, vb)):
            if dil == 1:
                out_ref[0] = jnp.concatenate(halves, axis=1).astype(BF16)
            else:
                _store_residue_major(halves, next(scratch), out_ref, dil)
    uc_ref[0] = section(OFF_UC, POOL_WIDTH)


def _proj(x, layer, gmix, w_main, ones_blk, gqa, gka, gqb, gkb, ca, sa, cb, sb):
    b, s, d = x.shape
    tm = PROJ_TM
    grid = (b, s // tm)
    row = lambda bi, i: (bi, i, 0)
    lay = lambda bi, i: (layer, 0, 0)
    tab = lambda bi, i: (i, 0)
    bsd = lambda w, dt: jax.ShapeDtypeStruct((b, s, w), dt)
    dils = [dil for _, dil in B_GROUP_CFG]
    out_shape = ([jax.ShapeDtypeStruct((b, A_Q_WIDTH, s), BF16), bsd(A_KV_WIDTH, BF16),
                  jax.ShapeDtypeStruct((b, A_KV_HEADS * A_VT_ROWS, s), BF16)]
                 + [jax.ShapeDtypeStruct((b, s // dil, dil * B_GROUP_WIDTH), BF16)
                    for dil in dils] * 3
                 + [bsd(POOL_WIDTH, F32)])
    out_specs = ([pl.BlockSpec((1, A_Q_WIDTH, tm), lambda bi, i: (bi, 0, i)),
                  pl.BlockSpec((1, tm, A_KV_WIDTH), row),
                  pl.BlockSpec((1, A_KV_HEADS * A_VT_ROWS, tm), lambda bi, i: (bi, 0, i))]
                 + [pl.BlockSpec((1, tm // dil, dil * B_GROUP_WIDTH), row) for dil in dils] * 3
                 + [pl.BlockSpec((1, tm, POOL_WIDTH), row)])
    n_perm = 3 * sum(dil > 1 for dil in dils)
    gain_spec = _resident((None, 1, LANES), lay)
    tab_spec = pl.BlockSpec((tm, LANES), tab)
    in_specs = [pl.BlockSpec((1, tm, d), row),
                _resident((None, 1, d), lay),
                _resident((None, d, MAIN_WIDTH), lay),
                _resident((MXU_DIM, MXU_DIM), lambda bi, i: (0, 0)),
                gain_spec, gain_spec, gain_spec, gain_spec,
                tab_spec, tab_spec, tab_spec, tab_spec]
    return pl.pallas_call(
        _proj_kernel, grid=grid, in_specs=in_specs, out_specs=out_specs, out_shape=out_shape,
        scratch_shapes=[pltpu.VMEM((2, tm, LANES), F32)] * n_perm,
        compiler_params=_params(2), name="proj",
    )(x, gmix, w_main, ones_blk, gqa, gka, gqb, gkb, ca, sa, cb, sb)


def _attn_a_kernel(qt_ref, k_ref, vt_ref, o_ref, s_ref, acc_ref, *, tq, tk, n_slots, ahead):
    n_chunks = k_ref.shape[1] // tk
    nq = A_GROUP * tq
    def q_operand(g):
        heads = [qt_ref[0, (g * A_GROUP + j) * HEAD_DIM:(g * A_GROUP + j + 1) * HEAD_DIM, :]
                 for j in range(A_GROUP)]
        qg = jnp.concatenate(heads, axis=1)
        zeros = jnp.zeros_like(qg)
        parts = [zeros] * A_KV_HEADS
        parts[g] = qg
        return jnp.concatenate(parts, axis=0)

    rhs = [q_operand(g) for g in range(A_KV_HEADS)]
    steps = [(g, c) for g in range(A_KV_HEADS) for c in range(n_chunks)]

    def scores(t):
        g, c = steps[t]
        sc = _dot(k_ref[0, c * tk:(c + 1) * tk, :], rhs[g])
        s_ref[t % n_slots] = sc
        return jnp.max(sc, axis=0, keepdims=True)

    def absorb(t, top, state):
        g, c = steps[t]
        m, acc = state
        m_new = jnp.maximum(m, top)
        alpha = jnp.exp2(m - m_new)
        p = jnp.exp2(s_ref[t % n_slots] - m_new)
        vt = vt_ref[0, g * A_VT_ROWS:(g + 1) * A_VT_ROWS, c * tk:(c + 1) * tk]
        return m_new, alpha * acc + _dot(vt, p.astype(BF16))

    tops = [scores(t) for t in range(ahead)]
    state = None
    for t, (g, c) in enumerate(steps):
        if t + ahead < len(steps):
            tops.append(scores(t + ahead))
        if c == 0:
            state = (jnp.full((1, nq), NEG_INF, F32), jnp.zeros((A_VT_ROWS, nq), F32))
        state = absorb(t, tops.pop(0), state)
        if c == n_chunks - 1:
            acc = state[1]
            o = acc[0:HEAD_DIM] / acc[HEAD_DIM:HEAD_DIM + 1]
            for j in range(A_GROUP):
                hd = (g * A_GROUP + j) * HEAD_DIM
                acc_ref[hd:hd + HEAD_DIM, :] = o[:, j * tq:(j + 1) * tq]
    o_ref[0] = acc_ref[...].T.astype(BF16)


def _attn_a(qat, ka, vat):
    b, _, s = qat.shape
    tq, tk = ATTN_A_TQ, ATTN_A_TK
    n_slots, ahead = ATTN_A_SLOTS, ATTN_A_AHEAD
    assert ahead < n_slots and s // tk >= ahead
    return pl.pallas_call(
        functools.partial(_attn_a_kernel, tq=tq, tk=tk, n_slots=n_slots, ahead=ahead),
        grid=(b, s // tq),
        in_specs=[pl.BlockSpec((1, A_Q_WIDTH, tq), lambda bi, i: (bi, 0, i)),
                  pl.BlockSpec((1, s, A_KV_WIDTH), lambda bi, i: (bi, 0, 0)),
                  pl.BlockSpec((1, A_KV_HEADS * A_VT_ROWS, s), lambda bi, i: (bi, 0, 0))],
        out_specs=pl.BlockSpec((1, tq, A_Q_WIDTH), lambda bi, i: (bi, i, 0)),
        out_shape=jax.ShapeDtypeStruct((b, s, A_Q_WIDTH), BF16),
        scratch_shapes=[pltpu.VMEM((n_slots, tk, A_GROUP * tq), F32),
                        pltpu.VMEM((A_Q_WIDTH, tq), F32)],
        compiler_params=_params(2), name="attn_a",
    )(qat, ka, vat)


def _attn_b_kernel(q_ref, k_ref, v_ref, o_ref, lse_ref, *, tl, half_span):
    sub_len = k_ref.shape[1]
    qc = ATTN_B_QC
    win = qc + 2 * half_span
    lt = pl.program_id(2)
    lane_head = lax.broadcasted_iota(jnp.int32, (1, B_GROUP_WIDTH), 1) // HEAD_DIM
    head_lanes = [lane_head == hh for hh in range(B_HEADS_PER_GROUP)]
    head_ones = [jnp.where(hl, 1.0, 0.0).astype(BF16) for hl in head_lanes]
    for c in range(tl // qc):
        row0 = lt * tl + c * qc
        ws = pl.multiple_of(jnp.clip(row0 - half_span, 0, sub_len - win), half_span)
        q = q_ref[0, c * qc:(c + 1) * qc, :]
        kw = k_ref[0, pl.ds(ws, win), :]
        vw = v_ref[0, pl.ds(ws, win), :]
        qpos = row0 + lax.broadcasted_iota(jnp.int32, (qc, win), 0)
        kpos = ws + lax.broadcasted_iota(jnp.int32, (qc, win), 1)
        valid = jnp.abs(qpos - kpos) <= half_span
        k_stack = jnp.concatenate([kw * ones for ones in head_ones], axis=0)
        v_stack = jnp.concatenate(
            [jnp.concatenate([vw * ones, jnp.broadcast_to(ones, vw.shape)], axis=1)
             for ones in head_ones], axis=0)
        sc_all = lax.dot_general(q, k_stack, (((1,), (1,)), ((), ())),
                                 preferred_element_type=F32)
        m_all = None
        probs = []
        for hh in range(B_HEADS_PER_GROUP):
            sc = jnp.where(valid, sc_all[:, hh * win:(hh + 1) * win], NEG_INF)
            m = jnp.max(sc, axis=-1, keepdims=True)
            probs.append(jnp.exp2(sc - m).astype(BF16))
            m_all = jnp.where(head_lanes[hh], m, 0.0 if m_all is None else m_all)
        ol = _dot(jnp.concatenate(probs, axis=1), v_stack)
        o_ref[0, c * qc:(c + 1) * qc, :] = ol[:, 0:B_GROUP_WIDTH] / ol[:, B_GROUP_WIDTH:]
        lse_ref[0, c * qc:(c + 1) * qc, :] = m_all + jnp.log2(ol[:, B_GROUP_WIDTH:])


def _attn_b(q, k, v, window, dilation):
    b, sub_len, _ = q.shape
    w = B_GROUP_WIDTH
    half_span = window // (2 * dilation)
    tl = min(ATTN_B_TL, sub_len)
    tile = pl.BlockSpec((1, tl, w), lambda bi, r, lt: (bi, lt, r))
    whole = pl.BlockSpec((1, sub_len, w), lambda bi, r, lt: (bi, 0, r))
    return pl.pallas_call(
        functools.partial(_attn_b_kernel, tl=tl, half_span=half_span),
        grid=(b, dilation, sub_len // tl),
        in_specs=[tile, whole, whole],
        out_specs=[tile, tile],
        out_shape=[jax.ShapeDtypeStruct((b, sub_len, dilation * w), F32)] * 2,
        compiler_params=_params(3), name=f"attn_b_d{dilation}",
    )(q, k, v)


def _merge_kernel(x_ref, oa_ref, ob0_ref, ob1_ref, ob2_ref, ls0_ref, ls1_ref, ls2_ref,
                  ucp_ref, uc_ref, ucn_ref,
                  gmix_ref, wg_ref, bg_ref, wa_ref, wb_ref, wc_ref, lin_ref, psc_ref, wo_ref,
                  out_ref, ext_ref, *perm_refs, tm, seq_len):
    i = pl.program_id(1)
    d = x_ref.shape[2]
    x = x_ref[0]
    h = _rms_norm_rows(x, gmix_ref[...]).astype(BF16)

    scratch = iter(perm_refs)
    token_major = lambda ref, dil: ref[0] if dil == 1 else _load_token_major(ref, next(scratch), dil)
    dils = [dil for _, dil in B_GROUP_CFG]
    obs = [token_major(ref, dil) for ref, dil in zip((ob0_ref, ob1_ref, ob2_ref), dils)]
    ls = [token_major(ref, dil) for ref, dil in zip((ls0_ref, ls1_ref, ls2_ref), dils)]
    top = jnp.maximum(jnp.maximum(ls[0], ls[1]), ls[2])
    wts = [jnp.exp2(t - top) for t in ls]
    ob = (wts[0] * obs[0] + wts[1] * obs[1] + wts[2] * obs[2]) / (wts[0] + wts[1] + wts[2])

    halo = POOL_HALO
    ext_ref[0:halo, :] = jnp.where(i > 0, ucp_ref[0], 0.0)
    ext_ref[halo:halo + tm, :] = uc_ref[0]
    ext_ref[halo + tm:2 * halo + tm, :] = jnp.where(i < pl.num_programs(1) - 1, ucn_ref[0], 0.0)
    t = i * tm + lax.broadcasted_iota(jnp.int32, (tm, 1), 0)
    mixed = []
    for gi, w in enumerate(POOL_WINDOWS):
        cols = slice(gi * POOL_GROUP_WIDTH, (gi + 1) * POOL_GROUP_WIDTH)
        tot = None
        for j in range(w):
            r0 = halo - w // 2 + j
            part = ext_ref[r0:r0 + tm, cols]
            tot = part if tot is None else tot + part
        cnt = jnp.minimum(t + (w - w // 2), seq_len) - jnp.maximum(t - w // 2, 0)
        pooled = tot / cnt.astype(F32) - ext_ref[halo:halo + tm, cols]
        mixed.append(_dot(pooled.astype(BF16), lin_ref[gi]) * psc_ref[:, cols])

    ya = _dot(oa_ref[0], wa_ref[...])
    yb = _dot(ob.astype(BF16), wb_ref[...])
    yc = _dot(jnp.concatenate(mixed, axis=1).astype(BF16), wc_ref[...])
    gz = _dot(h, wg_ref[...]) + bg_ref[...]
    gates = 1.0 / (1.0 + jnp.exp(-gz))
    merged = gates[:, 0:d] * ya + gates[:, d:2 * d] * yb + gates[:, 2 * d:3 * d] * yc
    out_ref[0] = x + _dot(merged.astype(BF16), wo_ref[...])


def _merge(x, layer, oa, obs, lss, uc, gmix, wg, bg, wa, wb, wc, lin, psc, wo):
    b, s, d = x.shape
    tm = MERGE_TM
    nt = s // tm
    hb = tm // POOL_HALO
    n_hblk = s // POOL_HALO
    row = lambda bi, i: (bi, i, 0)
    lay = lambda bi, i: (layer, 0, 0)
    tile = lambda w: pl.BlockSpec((1, tm, w), row)
    dils = [dil for _, dil in B_GROUP_CFG]
    b_tile = [pl.BlockSpec((1, tm // dil, dil * B_GROUP_WIDTH), row) for dil in dils]
    n_perm = 2 * sum(dil > 1 for dil in dils)
    in_specs = ([tile(d), tile(A_Q_WIDTH)] + b_tile * 2
                + [pl.BlockSpec((1, POOL_HALO, POOL_WIDTH),
                                lambda bi, i: (bi, jnp.maximum(i * hb - 1, 0), 0)),
                   tile(POOL_WIDTH),
                   pl.BlockSpec((1, POOL_HALO, POOL_WIDTH),
                                lambda bi, i: (bi, jnp.minimum((i + 1) * hb, n_hblk - 1), 0)),
                   _resident((None, 1, d), lay),
                   _resident((None, d, 3 * d), lay),
                   _resident((None, 1, 3 * d), lay),
                   _resident((None, A_Q_WIDTH, d), lay),
                   _resident((None, B_GROUP_WIDTH, d), lay),
                   _resident((None, POOL_WIDTH, d), lay),
                   _resident((None, len(POOL_WINDOWS), POOL_GROUP_WIDTH, POOL_GROUP_WIDTH),
                             lambda bi, i: (layer, 0, 0, 0)),
                   _resident((None, 1, POOL_WIDTH), lay),
                   _resident((None, d, d), lay)])
    return pl.pallas_call(
        functools.partial(_merge_kernel, tm=tm, seq_len=s),
        grid=(b, nt), in_specs=in_specs, out_specs=tile(d),
        out_shape=jax.ShapeDtypeStruct((b, s, d), F32),
        scratch_shapes=([pltpu.VMEM((tm + 2 * POOL_HALO, POOL_WIDTH), F32)]
                        + [pltpu.VMEM((2, tm, LANES), F32)] * n_perm),
        compiler_params=_params(2), name="merge",
    )(x, oa, *obs, *lss, uc, uc, uc, gmix, wg, bg, wa, wb, wc, lin, psc, wo)


def _ffn_kernel(x_ref, gn_ref, wgate_ref, wup_ref, wdown_ref, out_ref):
    x = x_ref[0]
    h = _rms_norm_rows(x, gn_ref[...]).astype(BF16)
    a = _dot(h, wgate_ref[...])
    u = _dot(h, wup_ref[...])
    act = (a / (1.0 + jnp.exp(-a))) * u
    out_ref[0] = x + _dot(act.astype(BF16), wdown_ref[...])


def _ffn(x, layer, gn, wgate, wup, wdown):
    b, s, d = x.shape
    dff = wgate.shape[2]
    tm = FFN_TM
    row = lambda bi, i: (bi, i, 0)
    lay = lambda bi, i: (layer, 0, 0)
    return pl.pallas_call(
        _ffn_kernel, grid=(b, s // tm),
        in_specs=[pl.BlockSpec((1, tm, d), row),
                  _resident((None, 1, d), lay),
                  _resident((None, d, dff), lay),
                  _resident((None, d, dff), lay),
                  _resident((None, dff, d), lay)],
        out_specs=pl.BlockSpec((1, tm, d), row),
        out_shape=jax.ShapeDtypeStruct((b, s, d), F32),
        compiler_params=_params(2), name="ffn",
    )(x, gn, wgate, wup, wdown)


def _rope_tables(s):
    def angles(pos, dim):
        inv = ROPE_THETA ** (-jnp.arange(0, dim, 2, dtype=F32) / dim)
        return pos.astype(F32)[:, None] * inv[None, :]

    t = jnp.arange(s)
    ang_row = angles(t // GRID_W, HEAD_DIM // 2)
    ang_col = angles(t % GRID_W, HEAD_DIM // 2)
    ang_seq = angles(t, HEAD_DIM)

    def table(parts):
        return jnp.tile(jnp.concatenate(parts, axis=1), (1, LANES // HEAD_DIM))

    ca = table([jnp.cos(ang_row)] * 2 + [jnp.cos(ang_col)] * 2)
    sa = table([-jnp.sin(ang_row), jnp.sin(ang_row), -jnp.sin(ang_col), jnp.sin(ang_col)])
    cb = table([jnp.cos(ang_seq)] * 2)
    sb = table([-jnp.sin(ang_seq), jnp.sin(ang_seq)])
    return ca, sa, cb, sb


def kernel(x, norm_mix, w_in, b_gate, qn_a, kn_a, qn_b, kn_b, pool_lin, pool_scale,
           w_branch_a, w_branch_b, w_branch_c, w_out, norm_ffn, w_ffn_gate, w_ffn_up, w_ffn_down):
    b, s, d = x.shape
    depth = w_in.shape[0]
    assert w_in.shape[2] == MAIN_WIDTH + 3 * d
    assert s % GRID_W == 0 and s % max(PROJ_TM, ATTN_A_TQ, ATTN_A_TK, MERGE_TM, FFN_TM) == 0
    for window, dil in B_GROUP_CFG:
        assert (s // dil) % ATTN_B_QC == 0 and s // dil >= ATTN_B_QC + window // dil

    ca, sa, cb, sb = _rope_tables(s)
    ones_blk = jnp.kron(jnp.eye(MXU_DIM // HEAD_DIM, dtype=F32),
                        jnp.ones((HEAD_DIM, HEAD_DIM), F32)).astype(BF16)
    row3 = lambda t: t.reshape(depth, 1, t.shape[-1])
    two_heads = lambda t: row3(jnp.tile(t, (1, LANES // HEAD_DIM)))
    w_main = w_in[:, :, :MAIN_WIDTH].astype(BF16)
    w_g = w_in[:, :, MAIN_WIDTH:].astype(BF16)
    gmix, gffn, bg, psc = row3(norm_mix), row3(norm_ffn), row3(b_gate), row3(pool_scale)
    gqa, gka, gqb, gkb = two_heads(qn_a), two_heads(kn_a), two_heads(qn_b), two_heads(kn_b)
    wa, wb, wc, wo = (t.astype(BF16) for t in (w_branch_a, w_branch_b, w_branch_c, w_out))
    lin = pool_lin.astype(BF16)
    wgate, wup, wdown = (t.astype(BF16) for t in (w_ffn_gate, w_ffn_up, w_ffn_down))

    for layer in range(depth):
        (qat, ka, vat, qb0, qb1, qb2, kb0, kb1, kb2, vb0, vb1, vb2, uc) = _proj(
            x, layer, gmix, w_main, ones_blk, gqa, gka, gqb, gkb, ca, sa, cb, sb)
        oa = _attn_a(qat, ka, vat)
        obs, lss = [], []
        for (window, dil), q, k, v in zip(B_GROUP_CFG, (qb0, qb1, qb2), (kb0, kb1, kb2),
                                          (vb0, vb1, vb2)):
            o, lse = _attn_b(q, k, v, window, dil)
            obs.append(o)
            lss.append(lse)
        x = _merge(x, layer, oa, obs, lss, uc, gmix, w_g, bg, wa, wb, wc, lin, psc, wo)
        x = _ffn(x, layer, gffn, wgate, wup, wdown)
    return x
```

```python
import functools
import math

import jax
import jax.numpy as jnp
from jax import lax
from jax.experimental import pallas as pl
from jax.experimental.pallas import tpu as pltpu

F32 = jnp.float32
BF16 = jnp.bfloat16

HEAD_DIM = 64
A_Q_HEADS = 8
A_KV_HEADS = 2
A_GROUP = A_Q_HEADS // A_KV_HEADS
B_GROUP_CFG = ((128, 1), (512, 4), (2048, 16))
B_HEADS_PER_GROUP = 4
B_GROUPS = len(B_GROUP_CFG)
POOL_WINDOWS = (2, 4, 8, 16)
POOL_GROUP_WIDTH = 128
GRID_W = 64
ROPE_THETA = 10000.0
EPS = 1e-6
NEG_INF = -1e30

A_Q_WIDTH = A_Q_HEADS * HEAD_DIM
A_KV_WIDTH = A_KV_HEADS * HEAD_DIM
B_GROUP_WIDTH = B_HEADS_PER_GROUP * HEAD_DIM
B_WIDTH = B_GROUPS * B_GROUP_WIDTH
POOL_WIDTH = len(POOL_WINDOWS) * POOL_GROUP_WIDTH
MAIN_WIDTH = A_Q_WIDTH + 2 * A_KV_WIDTH + 3 * B_WIDTH + POOL_WIDTH
OFF_QA = 0
OFF_KA = OFF_QA + A_Q_WIDTH
OFF_VA = OFF_KA + A_KV_WIDTH
OFF_QB = OFF_VA + A_KV_WIDTH
OFF_KB = OFF_QB + B_WIDTH
OFF_VB = OFF_KB + B_WIDTH
OFF_UC = OFF_VB + B_WIDTH

Q_SCALE = HEAD_DIM ** -0.5 * math.log2(math.e)

LANES = 128
SUBLANES = 8
BF16_ROWS = 16
MXU_DIM = 256
A_ONES_ROWS = 64
A_VT_ROWS = HEAD_DIM + A_ONES_ROWS
POOL_HALO = SUBLANES
VMEM_LIMIT_BYTES = 56 * 1024 * 1024

PROJ_TM = 512
ATTN_A_TQ = 256
ATTN_A_TK = 512
ATTN_A_SLOTS = 4
ATTN_A_AHEAD = 2
ATTN_B_TL = 512
ATTN_B_QC = 128
MERGE_TM = 512
FFN_TM = 512


def _params(n_axes):
    return pltpu.CompilerParams(dimension_semantics=("arbitrary",) * n_axes,
                                vmem_limit_bytes=VMEM_LIMIT_BYTES)


def _resident(block_shape, index_map):
    return pl.BlockSpec(block_shape, index_map, pipeline_mode=pl.Buffered(1))


def _rms_norm_rows(x, gain):
    ms = jnp.mean(x * x, axis=-1, keepdims=True)
    return (x * lax.rsqrt(ms + EPS)) * gain


def _dot(a, b):
    return jnp.dot(a, b, preferred_element_type=F32)


def _head_norm_rope(zs, ones_ref, gain, cos, sin, half, out_scale):
    tm, w = zs.shape
    lane = lax.broadcasted_iota(jnp.int32, (tm, LANES), 1)
    first = (lane & half) == 0
    pieces = []
    for c0 in range(0, w, MXU_DIM):
        cw = min(MXU_DIM, w - c0)
        zc = zs[:, c0:c0 + cw]
        z2 = zc * zc
        hi = z2.astype(BF16)
        lo = (z2 - hi.astype(F32)).astype(BF16)
        ones = ones_ref[:cw, :cw]
        ss = _dot(hi, ones) + _dot(lo, ones)
        r = lax.rsqrt(ss * (1.0 / HEAD_DIM) + EPS)
        for p0 in range(0, cw, LANES):
            xn = (zc[:, p0:p0 + LANES] * r[:, p0:p0 + LANES]) * gain
            partner = jnp.where(first, pltpu.roll(xn, LANES - half, 1), pltpu.roll(xn, half, 1))
            o = xn * cos + partner * sin
            if out_scale != 1.0:
                o = o * out_scale
            pieces.append(o)
    return pieces


def _store_residue_major(halves, scr_ref, out_ref, dil):
    n = halves[0].shape[0] // dil
    for hf, piece in enumerate(halves):
        scr_ref[hf] = piece
    for r in range(dil):
        for hf in range(2):
            c0 = r * B_GROUP_WIDTH + hf * LANES
            out_ref[0, :, c0:c0 + LANES] = scr_ref[hf, pl.ds(r, n, stride=dil), :].astype(BF16)


def _load_token_major(blk_ref, scr_ref, dil):
    n = blk_ref.shape[1]
    for r in range(dil):
        for hf in range(2):
            c0 = r * B_GROUP_WIDTH + hf * LANES
            scr_ref[hf, pl.ds(r, n, stride=dil), :] = blk_ref[0, :, c0:c0 + LANES]
    return jnp.concatenate([scr_ref[0], scr_ref[1]], axis=1)


def _proj_kernel(x_ref, gmix_ref, w_ref, ones_ref, gqa_ref, gka_ref, gqb_ref, gkb_ref,
                 ca_ref, sa_ref, cb_ref, sb_ref,
                 qat_ref, ka_ref, vat_ref,
                 qb0_ref, qb1_ref, qb2_ref, kb0_ref, kb1_ref, kb2_ref, vb0_ref, vb1_ref, vb2_ref,
                 uc_ref, *perm_refs):
    h = _rms_norm_rows(x_ref[0], gmix_ref[...]).astype(BF16)
    ca, sa, cb, sb = ca_ref[...], sa_ref[...], cb_ref[...], sb_ref[...]

    def section(c0, width):
        return _dot(h, w_ref[:, c0:c0 + width])

    qa = _head_norm_rope(section(OFF_QA, A_Q_WIDTH), ones_ref, gqa_ref[...], ca, sa,
                         HEAD_DIM // 4, Q_SCALE)
    for p, piece in enumerate(qa):
        qat_ref[0, p * LANES:(p + 1) * LANES, :] = piece.T.astype(BF16)
    zkv = section(OFF_KA, 2 * A_KV_WIDTH)
    ka = _head_norm_rope(zkv[:, 0:A_KV_WIDTH], ones_ref, gka_ref[...], ca, sa, HEAD_DIM // 4, 1.0)
    ka_ref[0] = ka[0].astype(BF16)
    vat = zkv[:, A_KV_WIDTH:2 * A_KV_WIDTH].T.astype(BF16)
    for g in range(A_KV_HEADS):
        r0 = g * A_VT_ROWS
        vat_ref[0, r0:r0 + HEAD_DIM, :] = vat[g * HEAD_DIM:(g + 1) * HEAD_DIM]
        vat_ref[0, r0 + HEAD_DIM:r0 + A_VT_ROWS, :] = jnp.ones((A_ONES_ROWS, vat.shape[1]), BF16)

    qb = _head_norm_rope(section(OFF_QB, B_WIDTH), ones_ref, gqb_ref[...], cb, sb,
                         HEAD_DIM // 2, Q_SCALE)
    kb = _head_norm_rope(section(OFF_KB, B_WIDTH), ones_ref, gkb_ref[...], cb, sb,
                         HEAD_DIM // 2, 1.0)
    zvb = section(OFF_VB, B_WIDTH)
    scratch = iter(perm_refs)
    for g, (q_ref, k_ref, v_ref) in enumerate(((qb0_ref, kb0_ref, vb0_ref),
                                               (qb1_ref, kb1_ref, vb1_ref),
                                               (qb2_ref, kb2_ref, vb2_ref))):
        dil = B_GROUP_CFG[g][1]
        c0 = g * B_GROUP_WIDTH
        vb = [zvb[:, c0:c0 + LANES], zvb[:, c0 + LANES:c0 + 2 * LANES]]
        for out_ref, halves in ((q_ref, qb[2 * g:2 * g + 2]), (k_ref, kb[2 * g:2 * g + 2]),
                                (v_ref, vb)):
            if dil == 1:
                out_ref[0] = jnp.concatenate(halves, axis=1).astype(BF16)
            else:
                _store_residue_major(halves, next(scratch), out_ref, dil)
    uc_ref[0] = section(OFF_UC, POOL_WIDTH)


def _proj(x, layer, gmix, w_main, ones_blk, gqa, gka, gqb, gkb, ca, sa, cb, sb):
    b, s, d = x.shape
    tm = PROJ_TM
    grid = (b, s // tm)
    row = lambda bi, i: (bi, i, 0)
    lay = lambda bi, i: (layer, 0, 0)
    tab = lambda bi, i: (i, 0)
    bsd = lambda w, dt: jax.ShapeDtypeStruct((b, s, w), dt)
    dils = [dil for _, dil in B_GROUP_CFG]
    out_shape = ([jax.ShapeDtypeStruct((b, A_Q_WIDTH, s), BF16), bsd(A_KV_WIDTH, BF16),
                  jax.ShapeDtypeStruct((b, A_KV_HEADS * A_VT_ROWS, s), BF16)]
                 + [jax.ShapeDtypeStruct((b, s // dil, dil * B_GROUP_WIDTH), BF16)
                    for dil in dils] * 3
                 + [bsd(POOL_WIDTH, F32)])
    out_specs = ([pl.BlockSpec((1, A_Q_WIDTH, tm), lambda bi, i: (bi, 0, i)),
                  pl.BlockSpec((1, tm, A_KV_WIDTH), row),
                  pl.BlockSpec((1, A_KV_HEADS * A_VT_ROWS, tm), lambda bi, i: (bi, 0, i))]
                 + [pl.BlockSpec((1, tm // dil, dil * B_GROUP_WIDTH), row) for dil in dils] * 3
                 + [pl.BlockSpec((1, tm, POOL_WIDTH), row)])
    n_perm = 3 * sum(dil > 1 for dil in dils)
    gain_spec = _resident((None, 1, LANES), lay)
    tab_spec = pl.BlockSpec((tm, LANES), tab)
    in_specs = [pl.BlockSpec((1, tm, d), row),
                _resident((None, 1, d), lay),
                _resident((None, d, MAIN_WIDTH), lay),
                _resident((MXU_DIM, MXU_DIM), lambda bi, i: (0, 0)),
                gain_spec, gain_spec, gain_spec, gain_spec,
                tab_spec, tab_spec, tab_spec, tab_spec]
    return pl.pallas_call(
        _proj_kernel, grid=grid, in_specs=in_specs, out_specs=out_specs, out_shape=out_shape,
        scratch_shapes=[pltpu.VMEM((2, tm, LANES), F32)] * n_perm,
        compiler_params=_params(2), name="proj",
    )(x, gmix, w_main, ones_blk, gqa, gka, gqb, gkb, ca, sa, cb, sb)


def _attn_a_kernel(qt_ref, k_ref, vt_ref, o_ref, acc_ref, *s_refs, tq, tk, n_slots, ahead):
    n_chunks = k_ref.shape[1] // tk
    nq = A_GROUP * tq
    def q_operand(g):
        heads = [qt_ref[0, (g * A_GROUP + j) * HEAD_DIM:(g * A_GROUP + j + 1) * HEAD_DIM, :]
                 for j in range(A_GROUP)]
        qg = jnp.concatenate(heads, axis=1)
        zeros = jnp.zeros_like(qg)
        parts = [zeros] * A_KV_HEADS
        parts[g] = qg
        return jnp.concatenate(parts, axis=0)

    rhs = [q_operand(g) for g in range(A_KV_HEADS)]
    zero = jnp.minimum(pl.program_id(1), 0)
    steps = [(g, c) for g in range(A_KV_HEADS) for c in range(n_chunks)]

    def scores(t):
        g, c = steps[t]
        sc = _dot(k_ref[0, c * tk:(c + 1) * tk, :], rhs[g])
        s_refs[t % n_slots][zero] = sc
        return jnp.max(sc, axis=0, keepdims=True)

    def absorb(t, top, state):
        g, c = steps[t]
        m, acc = state
        m_new = jnp.maximum(m, top)
        alpha = jnp.exp2(m - m_new)
        p = jnp.exp2(s_refs[t % n_slots][zero] - m_new)
        vt = vt_ref[0, g * A_VT_ROWS:(g + 1) * A_VT_ROWS, c * tk:(c + 1) * tk]
        return m_new, alpha * acc + _dot(vt, p.astype(BF16))

    tops = [scores(t) for t in range(ahead)]
    state = None
    for t, (g, c) in enumerate(steps):
        if t + ahead < len(steps):
            tops.append(scores(t + ahead))
        if c == 0:
            state = (jnp.full((1, nq), NEG_INF, F32), jnp.zeros((A_VT_ROWS, nq), F32))
        state = absorb(t, tops.pop(0), state)
        if c == n_chunks - 1:
            acc = state[1]
            o = acc[0:HEAD_DIM] / acc[HEAD_DIM:HEAD_DIM + 1]
            for j in range(A_GROUP):
                hd = (g * A_GROUP + j) * HEAD_DIM
                acc_ref[hd:hd + HEAD_DIM, :] = o[:, j * tq:(j + 1) * tq]
    o_ref[0] = acc_ref[...].T.astype(BF16)


def _attn_a(qat, ka, vat):
    b, _, s = qat.shape
    tq, tk = ATTN_A_TQ, ATTN_A_TK
    n_slots, ahead = ATTN_A_SLOTS, ATTN_A_AHEAD
    assert ahead < n_slots and s // tk >= ahead
    return pl.pallas_call(
        functools.partial(_attn_a_kernel, tq=tq, tk=tk, n_slots=n_slots, ahead=ahead),
        grid=(b, s // tq),
        in_specs=[pl.BlockSpec((1, A_Q_WIDTH, tq), lambda bi, i: (bi, 0, i)),
                  pl.BlockSpec((1, s, A_KV_WIDTH), lambda bi, i: (bi, 0, 0)),
                  pl.BlockSpec((1, A_KV_HEADS * A_VT_ROWS, s), lambda bi, i: (bi, 0, 0))],
        out_specs=pl.BlockSpec((1, tq, A_Q_WIDTH), lambda bi, i: (bi, i, 0)),
        out_shape=jax.ShapeDtypeStruct((b, s, A_Q_WIDTH), BF16),
        scratch_shapes=([pltpu.VMEM((A_Q_WIDTH, tq), F32)]
                        + [pltpu.VMEM((1, tk, A_GROUP * tq), F32)] * n_slots),
        compiler_params=_params(2), name="attn_a",
    )(qat, ka, vat)


def _attn_b_kernel(q_ref, k_ref, v_ref, o_ref, lse_ref, *, tl, half_span):
    sub_len = k_ref.shape[1]
    qc = ATTN_B_QC
    win = qc + 2 * half_span
    lt = pl.program_id(2)
    lane_head = lax.broadcasted_iota(jnp.int32, (1, B_GROUP_WIDTH), 1) // HEAD_DIM
    head_lanes = [lane_head == hh for hh in range(B_HEADS_PER_GROUP)]
    head_ones = [jnp.where(hl, 1.0, 0.0).astype(BF16) for hl in head_lanes]
    for c in range(tl // qc):
        row0 = lt * tl + c * qc
        ws = pl.multiple_of(jnp.clip(row0 - half_span, 0, sub_len - win), half_span)
        q = q_ref[0, c * qc:(c + 1) * qc, :]
        kw = k_ref[0, pl.ds(ws, win), :]
        vw = v_ref[0, pl.ds(ws, win), :]
        qpos = row0 + lax.broadcasted_iota(jnp.int32, (qc, win), 0)
        kpos = ws + lax.broadcasted_iota(jnp.int32, (qc, win), 1)
        valid = jnp.abs(qpos - kpos) <= half_span
        k_stack = jnp.concatenate([kw * ones for ones in head_ones], axis=0)
        v_stack = jnp.concatenate(
            [jnp.concatenate([vw * ones, jnp.broadcast_to(ones, vw.shape)], axis=1)
             for ones in head_ones], axis=0)
        sc_all = lax.dot_general(q, k_stack, (((1,), (1,)), ((), ())),
                                 preferred_element_type=F32)
        m_all = None
        probs = []
        for hh in range(B_HEADS_PER_GROUP):
            sc = jnp.where(valid, sc_all[:, hh * win:(hh + 1) * win], NEG_INF)
            m = jnp.max(sc, axis=-1, keepdims=True)
            probs.append(jnp.exp2(sc - m).astype(BF16))
            m_all = jnp.where(head_lanes[hh], m, 0.0 if m_all is None else m_all)
        ol = _dot(jnp.concatenate(probs, axis=1), v_stack)
        o_ref[0, c * qc:(c + 1) * qc, :] = ol[:, 0:B_GROUP_WIDTH] / ol[:, B_GROUP_WIDTH:]
        lse_ref[0, c * qc:(c + 1) * qc, :] = m_all + jnp.log2(ol[:, B_GROUP_WIDTH:])


def _attn_b(q, k, v, window, dilation):
    b, sub_len, _ = q.shape
    w = B_GROUP_WIDTH
    half_span = window // (2 * dilation)
    tl = min(ATTN_B_TL, sub_len)
    tile = pl.BlockSpec((1, tl, w), lambda bi, r, lt: (bi, lt, r))
    whole = pl.BlockSpec((1, sub_len, w), lambda bi, r, lt: (bi, 0, r))
    return pl.pallas_call(
        functools.partial(_attn_b_kernel, tl=tl, half_span=half_span),
        grid=(b, dilation, sub_len // tl),
        in_specs=[tile, whole, whole],
        out_specs=[tile, tile],
        out_shape=[jax.ShapeDtypeStruct((b, sub_len, dilation * w), F32)] * 2,
        compiler_params=_params(3), name=f"attn_b_d{dilation}",
    )(q, k, v)


def _merge_kernel(x_ref, oa_ref, ob0_ref, ob1_ref, ob2_ref, ls0_ref, ls1_ref, ls2_ref,
                  ucp_ref, uc_ref, ucn_ref,
                  gmix_ref, wg_ref, bg_ref, wa_ref, wb_ref, wc_ref, lin_ref, psc_ref, wo_ref,
                  out_ref, ext_ref, *perm_refs, tm, seq_len):
    i = pl.program_id(1)
    d = x_ref.shape[2]
    x = x_ref[0]
    h = _rms_norm_rows(x, gmix_ref[...]).astype(BF16)

    scratch = iter(perm_refs)
    token_major = lambda ref, dil: ref[0] if dil == 1 else _load_token_major(ref, next(scratch), dil)
    dils = [dil for _, dil in B_GROUP_CFG]
    obs = [token_major(ref, dil) for ref, dil in zip((ob0_ref, ob1_ref, ob2_ref), dils)]
    ls = [token_major(ref, dil) for ref, dil in zip((ls0_ref, ls1_ref, ls2_ref), dils)]
    top = jnp.maximum(jnp.maximum(ls[0], ls[1]), ls[2])
    wts = [jnp.exp2(t - top) for t in ls]
    ob = (wts[0] * obs[0] + wts[1] * obs[1] + wts[2] * obs[2]) / (wts[0] + wts[1] + wts[2])

    halo = POOL_HALO
    ext_ref[0:halo, :] = jnp.where(i > 0, ucp_ref[0], 0.0)
    ext_ref[halo:halo + tm, :] = uc_ref[0]
    ext_ref[halo + tm:2 * halo + tm, :] = jnp.where(i < pl.num_programs(1) - 1, ucn_ref[0], 0.0)
    t = i * tm + lax.broadcasted_iota(jnp.int32, (tm, 1), 0)
    mixed = []
    for gi, w in enumerate(POOL_WINDOWS):
        cols = slice(gi * POOL_GROUP_WIDTH, (gi + 1) * POOL_GROUP_WIDTH)
        tot = None
        for j in range(w):
            r0 = halo - w // 2 + j
            part = ext_ref[r0:r0 + tm, cols]
            tot = part if tot is None else tot + part
        cnt = jnp.minimum(t + (w - w // 2), seq_len) - jnp.maximum(t - w // 2, 0)
        pooled = tot / cnt.astype(F32) - ext_ref[halo:halo + tm, cols]
        mixed.append(_dot(pooled.astype(BF16), lin_ref[gi]) * psc_ref[:, cols])

    ya = _dot(oa_ref[0], wa_ref[...])
    yb = _dot(ob.astype(BF16), wb_ref[...])
    yc = _dot(jnp.concatenate(mixed, axis=1).astype(BF16), wc_ref[...])
    gz = _dot(h, wg_ref[...]) + bg_ref[...]
    gates = 1.0 / (1.0 + jnp.exp(-gz))
    merged = gates[:, 0:d] * ya + gates[:, d:2 * d] * yb + gates[:, 2 * d:3 * d] * yc
    out_ref[0] = x + _dot(merged.astype(BF16), wo_ref[...])


def _merge(x, layer, oa, obs, lss, uc, gmix, wg, bg, wa, wb, wc, lin, psc, wo):
    b, s, d = x.shape
    tm = MERGE_TM
    nt = s // tm
    hb = tm // POOL_HALO
    n_hblk = s // POOL_HALO
    row = lambda bi, i: (bi, i, 0)
    lay = lambda bi, i: (layer, 0, 0)
    tile = lambda w: pl.BlockSpec((1, tm, w), row)
    dils = [dil for _, dil in B_GROUP_CFG]
    b_tile = [pl.BlockSpec((1, tm // dil, dil * B_GROUP_WIDTH), row) for dil in dils]
    n_perm = 2 * sum(dil > 1 for dil in dils)
    in_specs = ([tile(d), tile(A_Q_WIDTH)] + b_tile * 2
                + [pl.BlockSpec((1, POOL_HALO, POOL_WIDTH),
                                lambda bi, i: (bi, jnp.maximum(i * hb - 1, 0), 0)),
                   tile(POOL_WIDTH),
                   pl.BlockSpec((1, POOL_HALO, POOL_WIDTH),
                                lambda bi, i: (bi, jnp.minimum((i + 1) * hb, n_hblk - 1), 0)),
                   _resident((None, 1, d), lay),
                   _resident((None, d, 3 * d), lay),
                   _resident((None, 1, 3 * d), lay),
                   _resident((None, A_Q_WIDTH, d), lay),
                   _resident((None, B_GROUP_WIDTH, d), lay),
                   _resident((None, POOL_WIDTH, d), lay),
                   _resident((None, len(POOL_WINDOWS), POOL_GROUP_WIDTH, POOL_GROUP_WIDTH),
                             lambda bi, i: (layer, 0, 0, 0)),
                   _resident((None, 1, POOL_WIDTH), lay),
                   _resident((None, d, d), lay)])
    return pl.pallas_call(
        functools.partial(_merge_kernel, tm=tm, seq_len=s),
        grid=(b, nt), in_specs=in_specs, out_specs=tile(d),
        out_shape=jax.ShapeDtypeStruct((b, s, d), F32),
        scratch_shapes=([pltpu.VMEM((tm + 2 * POOL_HALO, POOL_WIDTH), F32)]
                        + [pltpu.VMEM((2, tm, LANES), F32)] * n_perm),
        compiler_params=_params(2), name="merge",
    )(x, oa, *obs, *lss, uc, uc, uc, gmix, wg, bg, wa, wb, wc, lin, psc, wo)


def _ffn_kernel(x_ref, gn_ref, wgate_ref, wup_ref, wdown_ref, out_ref):
    x = x_ref[0]
    h = _rms_norm_rows(x, gn_ref[...]).astype(BF16)
    a = _dot(h, wgate_ref[...])
    u = _dot(h, wup_ref[...])
    act = (a / (1.0 + jnp.exp(-a))) * u
    out_ref[0] = x + _dot(act.astype(BF16), wdown_ref[...])


def _ffn(x, layer, gn, wgate, wup, wdown):
    b, s, d = x.shape
    dff = wgate.shape[2]
    tm = FFN_TM
    row = lambda bi, i: (bi, i, 0)
    lay = lambda bi, i: (layer, 0, 0)
    return pl.pallas_call(
        _ffn_kernel, grid=(b, s // tm),
        in_specs=[pl.BlockSpec((1, tm, d), row),
                  _resident((None, 1, d), lay),
                  _resident((None, d, dff), lay),
                  _resident((None, d, dff), lay),
                  _resident((None, dff, d), lay)],
        out_specs=pl.BlockSpec((1, tm, d), row),
        out_shape=jax.ShapeDtypeStruct((b, s, d), F32),
        compiler_params=_params(2), name="ffn",
    )(x, gn, wgate, wup, wdown)


def _rope_tables(s):
    def angles(pos, dim):
        inv = ROPE_THETA ** (-jnp.arange(0, dim, 2, dtype=F32) / dim)
        return pos.astype(F32)[:, None] * inv[None, :]

    t = jnp.arange(s)
    ang_row = angles(t // GRID_W, HEAD_DIM // 2)
    ang_col = angles(t % GRID_W, HEAD_DIM // 2)
    ang_seq = angles(t, HEAD_DIM)

    def table(parts):
        return jnp.tile(jnp.concatenate(parts, axis=1), (1, LANES // HEAD_DIM))

    ca = table([jnp.cos(ang_row)] * 2 + [jnp.cos(ang_col)] * 2)
    sa = table([-jnp.sin(ang_row), jnp.sin(ang_row), -jnp.sin(ang_col), jnp.sin(ang_col)])
    cb = table([jnp.cos(ang_seq)] * 2)
    sb = table([-jnp.sin(ang_seq), jnp.sin(ang_seq)])
    return ca, sa, cb, sb


def kernel(x, norm_mix, w_in, b_gate, qn_a, kn_a, qn_b, kn_b, pool_lin, pool_scale,
           w_branch_a, w_branch_b, w_branch_c, w_out, norm_ffn, w_ffn_gate, w_ffn_up, w_ffn_down):
    b, s, d = x.shape
    depth = w_in.shape[0]
    assert w_in.shape[2] == MAIN_WIDTH + 3 * d
    assert s % GRID_W == 0 and s % max(PROJ_TM, ATTN_A_TQ, ATTN_A_TK, MERGE_TM, FFN_TM) == 0
    for window, dil in B_GROUP_CFG:
        assert (s // dil) % ATTN_B_QC == 0 and s // dil >= ATTN_B_QC + window // dil

    ca, sa, cb, sb = _rope_tables(s)
    ones_blk = jnp.kron(jnp.eye(MXU_DIM // HEAD_DIM, dtype=F32),
                        jnp.ones((HEAD_DIM, HEAD_DIM), F32)).astype(BF16)
    row3 = lambda t: t.reshape(depth, 1, t.shape[-1])
    two_heads = lambda t: row3(jnp.tile(t, (1, LANES // HEAD_DIM)))
    w_main = w_in[:, :, :MAIN_WIDTH].astype(BF16)
    w_g = w_in[:, :, MAIN_WIDTH:].astype(BF16)
    gmix, gffn, bg, psc = row3(norm_mix), row3(norm_ffn), row3(b_gate), row3(pool_scale)
    gqa, gka, gqb, gkb = two_heads(qn_a), two_heads(kn_a), two_heads(qn_b), two_heads(kn_b)
    wa, wb, wc, wo = (t.astype(BF16) for t in (w_branch_a, w_branch_b, w_branch_c, w_out))
    lin = pool_lin.astype(BF16)
    wgate, wup, wdown = (t.astype(BF16) for t in (w_ffn_gate, w_ffn_up, w_ffn_down))

    for layer in range(depth):
        (qat, ka, vat, qb0, qb1, qb2, kb0, kb1, kb2, vb0, vb1, vb2, uc) = _proj(
            x, layer, gmix, w_main, ones_blk, gqa, gka, gqb, gkb, ca, sa, cb, sb)
        oa = _attn_a(qat, ka, vat)
        obs, lss = [], []
        for (window, dil), q, k, v in zip(B_GROUP_CFG, (qb0, qb1, qb2), (kb0, kb1, kb2),
                                          (vb0, vb1, vb2)):
            o, lse = _attn_b(q, k, v, window, dil)
            obs.append(o)
            lss.append(lse)
        x = _merge(x, layer, oa, obs, lss, uc, gmix, w_g, bg, wa, wb, wc, lin, psc, wo)
        x = _ffn(x, layer, gffn, wgate, wup, wdown)
    return x
```
